```python
import jax, jax.numpy as jnp
from jax import lax
import numpy as np

D_MODEL = 1024
BATCH = 4
SEQ = 8192
DEPTH = 4

N_META = 16
BLOCK_Q = 128
N_MIXERS = 2
EPS = 1e-6
FOX_HEADS = 16
FOX_HEAD_DIM = D_MODEL // FOX_HEADS
FOX_TAU_MIN = 4.0
FOX_TAU_MAX = 4096.0
MLA_HEADS = 16
MLA_NOPE_DIM = 64
MLA_ROPE_DIM = 32
MLA_V_DIM = 64
MLA_Q_RANK = D_MODEL // 4
MLA_KV_RANK = D_MODEL // 8
ROPE_THETA = 10000.0
D_FF = 2816
N_EXPERTS = 8
TOP_K = 2
MAX_POS_OFFSET = 1024

kernel_name = 'hybrid_fox_mla_moe_meta_trunk'


def rms_norm(x, g):
    xf = x.astype(jnp.float32)
    y = xf * lax.rsqrt(jnp.mean(xf * xf, axis=-1, keepdims=True) + EPS)
    return (y * g.astype(jnp.float32)).astype(x.dtype)


def rope_angles(pos):
    inv_freq = ROPE_THETA ** (-jnp.arange(0, MLA_ROPE_DIM, 2, dtype=jnp.float32) / MLA_ROPE_DIM)
    return pos.astype(jnp.float32)[..., None] * inv_freq


def rotary(x, angles):
    half = x.shape[-1] // 2
    xf = x.astype(jnp.float32)
    x1, x2 = xf[..., :half], xf[..., half:]
    cos, sin = jnp.cos(angles), jnp.sin(angles)
    return jnp.concatenate([x1 * cos - x2 * sin, x2 * cos + x1 * sin], axis=-1).astype(x.dtype)


def causal_block_attention(q, k, v, decay):
    B, H, L, dk = q.shape
    dv = v.shape[-1]
    scale = dk ** -0.5
    n_real = L - N_META
    n_blocks = n_real // BLOCK_Q

    def attend(q_blk, q_pos, k_, v_, c_q, c_k):
        s = jnp.einsum('bhqd,bhkd->bhqk', q_blk, k_).astype(jnp.float32) * scale
        if c_q is not None:
            s = s + c_q[..., :, None] - c_k[..., None, :]
        mask = q_pos[:, None] >= jnp.arange(k_.shape[2])[None, :]
        p = jax.nn.softmax(jnp.where(mask, s, -jnp.inf), axis=-1)
        return jnp.einsum('bhqk,bhkd->bhqd', p.astype(v_.dtype), v_)

    def to_blocks(a):
        a = a[:, :, N_META:]
        a = a.reshape(a.shape[:2] + (n_blocks, BLOCK_Q) + a.shape[3:])
        return jnp.moveaxis(a, 2, 0)

    meta_pos = jnp.arange(N_META)
    c_meta = None if decay is None else decay[:, :, :N_META]
    out_meta = attend(q[:, :, :N_META], meta_pos, k[:, :, :N_META], v[:, :, :N_META], c_meta, c_meta)

    q_blocks = to_blocks(q)
    pos_blocks = N_META + jnp.arange(n_real).reshape(n_blocks, BLOCK_Q)
    if decay is None:
        out = lax.map(lambda xs: attend(xs[0], xs[1], k, v, None, None), (q_blocks, pos_blocks))
    else:
        c_blocks = to_blocks(decay)
        out = lax.map(lambda xs: attend(xs[0], xs[1], k, v, xs[2], decay), (q_blocks, pos_blocks, c_blocks))
    out = jnp.moveaxis(out, 0, 2).reshape(B, H, n_real, dv)
    return jnp.concatenate([out_meta, out], axis=2)


def fox_mixer(h, mix_norm, w_in, b_f, q_norm, k_norm, w_out):
    B, L, _ = h.shape
    xn = rms_norm(h, mix_norm)
    proj = xn @ w_in
    q, k, v, gate = jnp.split(proj[..., :4 * D_MODEL], 4, axis=-1)
    f_logit = (proj[..., 4 * D_MODEL:] + b_f).astype(jnp.float32)
    heads = lambda a: a.reshape(B, L, FOX_HEADS, FOX_HEAD_DIM)
    q = rms_norm(heads(q), q_norm).transpose(0, 2, 1, 3)
    k = rms_norm(heads(k), k_norm).transpose(0, 2, 1, 3)
    v = heads(v).transpose(0, 2, 1, 3)
    decay = jnp.cumsum(jax.nn.log_sigmoid(f_logit), axis=1).transpose(0, 2, 1)
    o = causal_block_attention(q, k, v, decay)
    o = o.transpose(0, 2, 1, 3).reshape(B, L, D_MODEL) * jax.nn.sigmoid(gate)
    return o @ w_out


def mla_mixer(h, angles, mix_norm, w_in, q_a_norm, w_q_up, kv_a_norm, w_kv_up,
              q_norm, k_nope_norm, k_rope_norm, w_out):
    B, L, _ = h.shape
    xn = rms_norm(h, mix_norm)
    proj = xn @ w_in
    c_q = rms_norm(proj[..., :MLA_Q_RANK], q_a_norm)
    c_kv = rms_norm(proj[..., MLA_Q_RANK:MLA_Q_RANK + MLA_KV_RANK], kv_a_norm)
    k_rope = proj[..., MLA_Q_RANK + MLA_KV_RANK:]

    q = (c_q @ w_q_up).reshape(B, L, MLA_HEADS, MLA_NOPE_DIM + MLA_ROPE_DIM)
    q = rms_norm(q, q_norm)
    q = jnp.concatenate([q[..., :MLA_NOPE_DIM], rotary(q[..., MLA_NOPE_DIM:], angles[:, :, None, :])], axis=-1)

    kv = (c_kv @ w_kv_up).reshape(B, L, MLA_HEADS, MLA_NOPE_DIM + MLA_V_DIM)
    k_nope = rms_norm(kv[..., :MLA_NOPE_DIM], k_nope_norm)
    v = kv[..., MLA_NOPE_DIM:]
    k_rope = rotary(rms_norm(k_rope, k_rope_norm), angles)
    k = jnp.concatenate([k_nope, jnp.broadcast_to(k_rope[:, :, None, :], (B, L, MLA_HEADS, MLA_ROPE_DIM))], axis=-1)

    o = causal_block_attention(q.transpose(0, 2, 1, 3), k.transpose(0, 2, 1, 3), v.transpose(0, 2, 1, 3), None)
    o = o.transpose(0, 2, 1, 3).reshape(B, L, MLA_HEADS * MLA_V_DIM)
    return o @ w_out


def dense_swiglu(h, ffn_norm, w_in, w_out):
    xn = rms_norm(h, ffn_norm)
    g, u = jnp.split(xn @ w_in, 2, axis=-1)
    return (jax.nn.silu(g) * u) @ w_out


def moe_swiglu(h, ffn_norm, w_router, w_in, w_out):
    xn = rms_norm(h, ffn_norm)
    logits = (xn @ w_router).astype(jnp.float32)
    top_vals, top_idx = lax.top_k(logits, TOP_K)
    top_w = jax.nn.softmax(top_vals, axis=-1)
    combine = jnp.sum(jax.nn.one_hot(top_idx, N_EXPERTS, dtype=jnp.float32) * top_w[..., None], axis=-2)
    combine = combine.astype(h.dtype)
    y = jnp.zeros_like(h)
    for e in range(N_EXPERTS):
        g, u = jnp.split(xn @ w_in[e], 2, axis=-1)
        y = y + combine[..., e:e + 1] * ((jax.nn.silu(g) * u) @ w_out[e])
    return y


def setup_inputs(seed: int = 0) -> dict:
    key = jax.random.key(seed)
    ks = iter(jax.random.split(key, 128))
    nk = lambda: next(ks)
    dense = lambda shape, fan_in: jax.random.normal(nk(), shape, jnp.float32) * fan_in ** -0.5
    gain = lambda n: 1.0 + 0.02 * jax.random.normal(nk(), (n,), jnp.float32)

    inp = {}
    inp['x'] = jax.random.normal(nk(), (BATCH, SEQ, D_MODEL), jnp.float32)
    offset = jax.random.randint(nk(), (BATCH, 1), 0, MAX_POS_OFFSET, dtype=jnp.int32)
    inp['positions'] = (N_META + offset + jnp.arange(SEQ, dtype=jnp.int32)[None, :]).astype(jnp.int32)
    inp['meta_tokens'] = jax.random.normal(nk(), (N_META, D_MODEL), jnp.float32)

    tau = jnp.geomspace(FOX_TAU_MIN, FOX_TAU_MAX, FOX_HEADS).astype(jnp.float32)
    for i in range(DEPTH):
        p = 'l%d_' % i
        inp[p + 'mix_norm'] = gain(D_MODEL)
        if i % N_MIXERS == 0:
            inp[p + 'fox_w_in'] = dense((D_MODEL, 4 * D_MODEL + FOX_HEADS), D_MODEL)
            inp[p + 'fox_b_f'] = jnp.log(tau - 1.0) + 0.01 * jax.random.normal(nk(), (FOX_HEADS,), jnp.float32)
            inp[p + 'fox_q_norm'] = gain(FOX_HEAD_DIM)
            inp[p + 'fox_k_norm'] = gain(FOX_HEAD_DIM)
            inp[p + 'fox_w_out'] = dense((D_MODEL, D_MODEL), D_MODEL)
        else:
            inp[p + 'mla_w_in'] = dense((D_MODEL, MLA_Q_RANK + MLA_KV_RANK + MLA_ROPE_DIM), D_MODEL)
            inp[p + 'mla_q_a_norm'] = gain(MLA_Q_RANK)
            inp[p + 'mla_w_q_up'] = dense((MLA_Q_RANK, MLA_HEADS * (MLA_NOPE_DIM + MLA_ROPE_DIM)), MLA_Q_RANK)
            inp[p + 'mla_kv_a_norm'] = gain(MLA_KV_RANK)
            inp[p + 'mla_w_kv_up'] = dense((MLA_KV_RANK, MLA_HEADS * (MLA_NOPE_DIM + MLA_V_DIM)), MLA_KV_RANK)
            inp[p + 'mla_q_norm'] = gain(MLA_NOPE_DIM + MLA_ROPE_DIM)
            inp[p + 'mla_k_nope_norm'] = gain(MLA_NOPE_DIM)
            inp[p + 'mla_k_rope_norm'] = gain(MLA_ROPE_DIM)
            inp[p + 'mla_w_out'] = dense((MLA_HEADS * MLA_V_DIM, D_MODEL), MLA_HEADS * MLA_V_DIM)
        inp[p + 'ffn_norm'] = gain(D_MODEL)
        if i % 2 == 0:
            inp[p + 'ffn_w_in'] = dense((D_MODEL, 2 * D_FF), D_MODEL)
            inp[p + 'ffn_w_out'] = dense((D_FF, D_MODEL), D_FF)
        else:
            inp[p + 'moe_w_router'] = dense((D_MODEL, N_EXPERTS), D_MODEL)
            inp[p + 'moe_w_in'] = dense((N_EXPERTS, D_MODEL, 2 * D_FF), D_MODEL)
            inp[p + 'moe_w_out'] = dense((N_EXPERTS, D_FF, D_MODEL), D_FF)
    return inp


def reference(x, positions, meta_tokens,
              l0_mix_norm, l0_fox_w_in, l0_fox_b_f, l0_fox_q_norm, l0_fox_k_norm, l0_fox_w_out,
              l0_ffn_norm, l0_ffn_w_in, l0_ffn_w_out,
              l1_mix_norm, l1_mla_w_in, l1_mla_q_a_norm, l1_mla_w_q_up, l1_mla_kv_a_norm, l1_mla_w_kv_up,
              l1_mla_q_norm, l1_mla_k_nope_norm, l1_mla_k_rope_norm, l1_mla_w_out,
              l1_ffn_norm, l1_moe_w_router, l1_moe_w_in, l1_moe_w_out,
              l2_mix_norm, l2_fox_w_in, l2_fox_b_f, l2_fox_q_norm, l2_fox_k_norm, l2_fox_w_out,
              l2_ffn_norm, l2_ffn_w_in, l2_ffn_w_out,
              l3_mix_norm, l3_mla_w_in, l3_mla_q_a_norm, l3_mla_w_q_up, l3_mla_kv_a_norm, l3_mla_w_kv_up,
              l3_mla_q_norm, l3_mla_k_nope_norm, l3_mla_k_rope_norm, l3_mla_w_out,
              l3_ffn_norm, l3_moe_w_router, l3_moe_w_in, l3_moe_w_out):
    B = x.shape[0]
    h = jnp.concatenate([jnp.broadcast_to(meta_tokens.astype(x.dtype)[None], (B, N_META, D_MODEL)), x], axis=1)
    pos = jnp.concatenate([jnp.broadcast_to(jnp.arange(N_META, dtype=positions.dtype)[None], (B, N_META)), positions], axis=1)
    angles = rope_angles(pos)

    mixer_params = [
        (l0_mix_norm, l0_fox_w_in, l0_fox_b_f, l0_fox_q_norm, l0_fox_k_norm, l0_fox_w_out),
        (l1_mix_norm, l1_mla_w_in, l1_mla_q_a_norm, l1_mla_w_q_up, l1_mla_kv_a_norm, l1_mla_w_kv_up,
         l1_mla_q_norm, l1_mla_k_nope_norm, l1_mla_k_rope_norm, l1_mla_w_out),
        (l2_mix_norm, l2_fox_w_in, l2_fox_b_f, l2_fox_q_norm, l2_fox_k_norm, l2_fox_w_out),
        (l3_mix_norm, l3_mla_w_in, l3_mla_q_a_norm, l3_mla_w_q_up, l3_mla_kv_a_norm, l3_mla_w_kv_up,
         l3_mla_q_norm, l3_mla_k_nope_norm, l3_mla_k_rope_norm, l3_mla_w_out),
    ]
    ffn_params = [
        (l0_ffn_norm, l0_ffn_w_in, l0_ffn_w_out),
        (l1_ffn_norm, l1_moe_w_router, l1_moe_w_in, l1_moe_w_out),
        (l2_ffn_norm, l2_ffn_w_in, l2_ffn_w_out),
        (l3_ffn_norm, l3_moe_w_router, l3_moe_w_in, l3_moe_w_out),
    ]

    for i in range(DEPTH):
        if i % N_MIXERS == 0:
            h = h + fox_mixer(h, *mixer_params[i])
        else:
            h = h + mla_mixer(h, angles, *mixer_params[i])
        if i % 2 == 0:
            h = h + dense_swiglu(h, *ffn_params[i])
        else:
            h = h + moe_swiglu(h, *ffn_params[i])
    return h[:, N_META:]
```

```python
import functools
import math

import jax
import jax.numpy as jnp
from jax import lax
from jax.experimental import pallas as pl
from jax.experimental.pallas import tpu as pltpu

D_MODEL = 1024
N_META = 16
EPS = 1e-6
N_HEADS = 16
HEAD_DIM = 64
MLA_NOPE = 64
MLA_ROPE = 32
MLA_Q_RANK = 256
MLA_KV_RANK = 128
ROPE_THETA = 10000.0
D_FF = 2816
N_EXPERTS = 8

LANES = 128
META_PAD = 512
TOKEN_TILE = 512
Q_TILE = 512
KV_TILE = 512
FF_CHUNK = 1408
GROUP_TILE = 512
ROUTE_ROWS = 256
LOG2E = 1.4426950408889634
NEG_BIG = -1e30
VMEM_LIMIT = 56 * 1024 * 1024

F32 = jnp.float32
BF16 = jnp.bfloat16


def _cparams(sem):
    return pltpu.CompilerParams(dimension_semantics=sem, vmem_limit_bytes=VMEM_LIMIT)


def _resident(shape):
    nd = len(shape)
    return pl.BlockSpec(shape, lambda *_: (0,) * nd, pipeline_mode=pl.Buffered(1))


def _rms_rows(x, gain, count):
    ms = jnp.sum(x * x, axis=-1, keepdims=True) * (1.0 / count)
    return x * lax.rsqrt(ms + EPS) * gain


def _lane_iota(shape):
    return lax.broadcasted_iota(jnp.int32, shape, len(shape) - 1)


def _dot(a, b):
    return jnp.dot(a, b, preferred_element_type=F32)


def _dot_nt(a, b):
    return lax.dot_general(a, b, (((1,), (1,)), ((), ())), preferred_element_type=F32)


def _split3(x):
    a = x.astype(BF16).astype(F32)
    b = (x - a).astype(BF16).astype(F32)
    c = (x - a - b).astype(BF16).astype(F32)
    return a, b, c


def _fox_pre_kernel(h_ref, g_ref, w_ref, bf_ref, qg_ref, kg_ref,
                    q_out, k_out, v_out, sg_out, carry_ref, *, tiles_per_seq, n_real_tiles):
    t = pl.program_id(0)
    tm = h_ref.shape[0]
    hp = N_HEADS * LANES
    xn = _rms_rows(h_ref[...], g_ref[...], D_MODEL).astype(BF16)

    f = _dot(xn, w_ref[:, 3 * hp + D_MODEL:]) + bf_ref[...]
    ls = jnp.minimum(f, 0.0) - jnp.log(1.0 + jnp.exp(-jnp.abs(f)))
    row = lax.broadcasted_iota(jnp.int32, (tm, LANES), 0)
    c = ls
    d = 1
    while d < tm:
        c = c + jnp.where(row >= d, pltpu.roll(c, d, axis=0), 0.0)
        d *= 2
    is_meta = t == n_real_tiles

    @pl.when(jnp.logical_or(t % tiles_per_seq == 0, is_meta))
    def _():
        carry_ref[...] = jnp.zeros_like(carry_ref)

    c = c + carry_ref[...]
    carry_ref[...] = c[tm - 1:tm, :]
    c_last_meta = jnp.sum(jnp.where(row == N_META - 1, c, 0.0), axis=0, keepdims=True)
    c = jnp.where(is_meta, c - c_last_meta, c) * LOG2E
    row1 = lax.broadcasted_iota(jnp.int32, (tm, 1), 0)
    row_valid = jnp.logical_or(jnp.logical_not(is_meta), row1 < N_META)

    lane = _lane_iota((1, LANES))
    feat = lane < HEAD_DIM
    qg = qg_ref[...]
    kg = kg_ref[...]
    pair = 2 * LANES
    for h in range(N_HEADS):
        sl = slice(h * LANES, (h + 1) * LANES)
        if h % 2 == 0:
            ps = slice(h * LANES, h * LANES + pair)
            q = _dot(xn, w_ref[:, ps])
            k = _dot(xn, w_ref[:, hp + h * LANES:hp + h * LANES + pair])
            v = _dot(xn, w_ref[:, 2 * hp + h * LANES:2 * hp + h * LANES + pair])
        hs = slice((h % 2) * LANES, (h % 2 + 1) * LANES)
        ch = jnp.sum(jnp.where(lane == h, c, 0.0), axis=-1, keepdims=True)
        c1, c2, c3 = _split3(ch)
        qh = _rms_rows(q[:, hs], qg, HEAD_DIM)
        kh = _rms_rows(k[:, hs], kg, HEAD_DIM)
        q_extra = jnp.where(lane == 64, c1, jnp.where(lane == 65, c2, jnp.where(lane == 66, c3,
                  jnp.where(lane < 70, 1.0, 0.0))))
        k_extra = jnp.where(lane < 67, 1.0,
                  jnp.where(lane == 67, jnp.where(row_valid, -c1, NEG_BIG),
                  jnp.where(lane == 68, -c2, jnp.where(lane == 69, -c3, 0.0))))
        q_out[:, sl] = jnp.where(feat, qh, q_extra).astype(BF16)
        k_out[:, sl] = jnp.where(feat, kh, k_extra).astype(BF16)
        v_out[:, sl] = jnp.where(lane == HEAD_DIM, 1.0, v[:, hs]).astype(BF16)
    gate = _dot(xn, w_ref[:, 3 * hp:3 * hp + D_MODEL])
    sg_out[...] = jax.nn.sigmoid(gate).astype(BF16)


def _fox_pre(h, mix_norm, w_all, b_f, qg, kg, *, seq):
    nt = h.shape[0]
    tm = TOKEN_TILE
    hp = N_HEADS * LANES
    n_real_tiles = (nt - META_PAD) // tm
    row = lambda w: pl.BlockSpec((tm, w), lambda t: (t, 0))
    return pl.pallas_call(
        functools.partial(_fox_pre_kernel, tiles_per_seq=seq // tm, n_real_tiles=n_real_tiles),
        grid=(nt // tm,),
        in_specs=[row(D_MODEL), _resident((1, D_MODEL)), _resident(w_all.shape),
                  _resident((1, LANES)), _resident((1, LANES)), _resident((1, LANES))],
        out_specs=[row(hp), row(hp), row(hp), row(D_MODEL)],
        out_shape=[jax.ShapeDtypeStruct((nt, hp), BF16)] * 3 + [jax.ShapeDtypeStruct((nt, D_MODEL), BF16)],
        scratch_shapes=[pltpu.VMEM((1, LANES), F32)],
        compiler_params=_cparams(("arbitrary",)),
        name="fox_pre",
    )(h, mix_norm, w_all, b_f, qg, kg)


def _rope_tables_kernel(pos_ref, freq_ref, sign_ref, cos_out, sin_out):
    ang = pos_ref[...] * freq_ref[...]
    cos_out[...] = jnp.cos(ang)
    sin_out[...] = jnp.sin(ang) * sign_ref[...]


def _rope_tables(pos_b, freq_p, sign_p):
    nt = pos_b.shape[0]
    tm = TOKEN_TILE
    row = pl.BlockSpec((tm, LANES), lambda t: (t, 0))
    return pl.pallas_call(
        _rope_tables_kernel,
        grid=(nt // tm,),
        in_specs=[row, _resident((1, LANES)), _resident((1, LANES))],
        out_specs=[row, row],
        out_shape=[jax.ShapeDtypeStruct((nt, LANES), F32)] * 2,
        compiler_params=_cparams(("arbitrary",)),
        name="rope_tables",
    )(pos_b, freq_p, sign_p)


def _rotate(x, cos_p, sin_p, lane):
    first_half = jnp.logical_and(lane >= MLA_NOPE, lane < MLA_NOPE + MLA_ROPE // 2)
    partner = jnp.where(first_half, pltpu.roll(x, LANES - MLA_ROPE // 2, axis=1),
                        pltpu.roll(x, MLA_ROPE // 2, axis=1))
    return x * cos_p + partner * sin_p


def _mla_pre_kernel(h_ref, g_ref, win_ref, qag_ref, kvag_ref, wq_ref, wkv_ref,
                    qg_ref, kng_ref, krg_ref, cos_ref, sin_ref,
                    q_out, k_out, v_out, *, n_real_tiles):
    t = pl.program_id(0)
    tm = h_ref.shape[0]
    xn = _rms_rows(h_ref[...], g_ref[...], D_MODEL).astype(BF16)
    proj = _dot(xn, win_ref[...])
    c_q = _rms_rows(proj[:, :MLA_Q_RANK], qag_ref[...], MLA_Q_RANK).astype(BF16)
    c_kv = _rms_rows(proj[:, MLA_Q_RANK:MLA_Q_RANK + MLA_KV_RANK], kvag_ref[...], MLA_KV_RANK).astype(BF16)
    lane = _lane_iota((1, LANES))
    cos_p = cos_ref[...]
    sin_p = sin_ref[...]
    row = lax.broadcasted_iota(jnp.int32, (tm, 1), 0)
    row_valid = jnp.logical_or(t != n_real_tiles, row < N_META)

    kr = _rms_rows(proj[:, MLA_Q_RANK + MLA_KV_RANK:], krg_ref[...], MLA_ROPE)
    kr = _rotate(kr, cos_p, sin_p, lane)
    kr = jnp.where(lane == MLA_NOPE + MLA_ROPE, jnp.where(row_valid, 0.0, NEG_BIG), kr)

    qg = qg_ref[...]
    kng = kng_ref[...]
    nope = lane < MLA_NOPE
    pair = 2 * LANES
    for h in range(N_HEADS):
        sl = slice(h * LANES, (h + 1) * LANES)
        if h % 2 == 0:
            q = _dot(c_q, wq_ref[:, h * LANES:h * LANES + pair])
            kv = _dot(c_kv, wkv_ref[:, h * LANES:h * LANES + pair])
        hs = slice((h % 2) * LANES, (h % 2 + 1) * LANES)
        qh = _rms_rows(q[:, hs], qg, MLA_NOPE + MLA_ROPE)
        qh = _rotate(qh, cos_p, sin_p, lane)
        q_out[:, sl] = jnp.where(lane == MLA_NOPE + MLA_ROPE, 1.0, qh).astype(BF16)
        kvh = kv[:, hs]
        kn = _rms_rows(jnp.where(nope, kvh, 0.0), kng, MLA_NOPE)
        k_out[:, sl] = jnp.where(nope, kn, kr).astype(BF16)
        vh = pltpu.roll(kvh, LANES // 2, axis=1)
        v_out[:, sl] = jnp.where(nope, vh, jnp.where(lane == MLA_NOPE, 1.0, 0.0)).astype(BF16)


def _mla_pre(h, mix_norm, w_in, qag, kvag, wq, wkv, qg, kng, krg, cos_p, sin_p):
    nt = h.shape[0]
    tm = TOKEN_TILE
    hp = N_HEADS * LANES
    row = lambda w: pl.BlockSpec((tm, w), lambda t: (t, 0))
    res = lambda a: _resident(a.shape)
    return pl.pallas_call(
        functools.partial(_mla_pre_kernel, n_real_tiles=(nt - META_PAD) // tm),
        grid=(nt // tm,),
        in_specs=[row(D_MODEL), res(mix_norm), res(w_in), res(qag), res(kvag), res(wq), res(wkv),
                  res(qg), res(kng), res(krg), row(LANES), row(LANES)],
        out_specs=[row(hp), row(hp), row(hp)],
        out_shape=[jax.ShapeDtypeStruct((nt, hp), BF16)] * 3,
        compiler_params=_cparams(("arbitrary",)),
        name="mla_pre",
    )(h, mix_norm, w_in, qag, kvag, wq, wkv, qg, kng, krg, cos_p, sin_p)


def _attn_kernel(*refs, tq, tk, batch):
    meta_queries = pl.program_id(0) == batch
    i = pl.program_id(2)

    @pl.when(jnp.logical_or(jnp.logical_not(meta_queries), i == 0))
    def _():
        _attn_block(*refs, tq=tq, tk=tk, i=i, meta_queries=meta_queries)


def _attn_block(q_ref, k_ref, v_ref, km_ref, vm_ref, o_ref, m_sc, acc_sc, s_sc, *, tq, tk, i, meta_queries):
    lane = _lane_iota((1, LANES))
    heads = [slice(a * LANES, (a + 1) * LANES) for a in range(2)]
    q = [q_ref[:, sl] for sl in heads]

    for a, sl in enumerate(heads):
        s = _dot_nt(q[a], km_ref[:, sl])
        r = lax.broadcasted_iota(jnp.int32, s.shape, 0)
        cidx = lax.broadcasted_iota(jnp.int32, s.shape, 1)
        s = jnp.where(jnp.logical_or(jnp.logical_not(meta_queries), cidx <= r), s, NEG_BIG)
        m = jnp.max(s, axis=-1, keepdims=True)
        p = jnp.exp2(s - m)
        m_sc[a] = jnp.broadcast_to(m, (m.shape[0], LANES))
        acc_sc[a] = _dot(p.astype(BF16), vm_ref[:, sl])

    def scores(blk, slot):
        start = pl.multiple_of(blk * tk, tk)
        for a, sl in enumerate(heads):
            s_sc[slot, a] = _dot_nt(q[a], k_ref[pl.ds(start, tk), sl])

    def update(blk, slot, mask=None):
        start = pl.multiple_of(blk * tk, tk)
        for a, sl in enumerate(heads):
            s = s_sc[slot, a]
            if mask is not None:
                s = jnp.where(mask, s, NEG_BIG)
            m = m_sc[a]
            m_new = jnp.maximum(m, jnp.max(s, axis=-1, keepdims=True))
            alpha = jnp.exp2(m - m_new)
            p = jnp.exp2(s - jnp.concatenate([m_new] * (tk // LANES), axis=1))
            m_sc[a] = m_new
            acc_sc[a] = alpha * acc_sc[a] + _dot(p.astype(BF16), v_ref[pl.ds(start, tk), sl])

    @pl.when(jnp.logical_not(meta_queries))
    def _():
        assert tq == tk
        scores(0, 0)

        def pair(p, _):
            scores(2 * p + 1, 1)
            update(2 * p, 0)
            scores(2 * p + 2, 0)
            update(2 * p + 1, 1)
            return 0

        lax.fori_loop(0, i // 2, pair, 0)
        r = lax.broadcasted_iota(jnp.int32, (tq, tk), 0)
        cidx = lax.broadcasted_iota(jnp.int32, (tq, tk), 1)

        @pl.when(i % 2 == 1)
        def _():
            scores(i, 1)
            update(i - 1, 0)
            update(i, 1, mask=cidx <= r)

        @pl.when(i % 2 == 0)
        def _():
            update(i, 0, mask=cidx <= r)
    outs = []
    for a in range(2):
        acc = acc_sc[a]
        denom = jnp.sum(jnp.where(lane == HEAD_DIM, acc, 0.0), axis=-1, keepdims=True)
        outs.append(acc / denom)
    packed = jnp.where(lane < HEAD_DIM, outs[0], pltpu.roll(outs[1], HEAD_DIM, axis=1))
    o_ref[...] = packed.astype(BF16)


def _attention(q2, k2, v2, *, batch, seq):
    nt = q2.shape[0]
    n_real = batch * seq
    tq, tk = Q_TILE, KV_TILE
    nq = seq // tq
    pair = 2 * LANES
    meta_blk = n_real // LANES
    assert tq == META_PAD
    q_blk = lambda b, hp, i: (jnp.where(b == batch, batch * nq, b * nq + i), hp)
    kv_blk = lambda b, hp, i: (jnp.minimum(b, batch - 1), hp)
    return pl.pallas_call(
        functools.partial(_attn_kernel, tq=tq, tk=tk, batch=batch),
        grid=(batch + 1, N_HEADS // 2, nq),
        in_specs=[pl.BlockSpec((tq, pair), q_blk),
                  pl.BlockSpec((seq, pair), kv_blk),
                  pl.BlockSpec((seq, pair), kv_blk),
                  pl.BlockSpec((LANES, pair), lambda b, hp, i: (meta_blk, hp)),
                  pl.BlockSpec((LANES, pair), lambda b, hp, i: (meta_blk, hp))],
        out_specs=pl.BlockSpec((tq, LANES), q_blk),
        out_shape=jax.ShapeDtypeStruct((nt, D_MODEL), BF16),
        scratch_shapes=[pltpu.VMEM((2, tq, LANES), F32), pltpu.VMEM((2, tq, LANES), F32),
                        pltpu.VMEM((2, 2, tq, tk), F32)],
        compiler_params=_cparams(("arbitrary", "arbitrary", "arbitrary")),
        name="attention",
    )(q2, k2, v2, k2, v2)


def _mixer_residual(h_ref, o_ref, sg_ref, wo_ref):
    a = o_ref[...]
    if sg_ref is not None:
        a = a * sg_ref[...]
    return h_ref[...] + _dot(a, wo_ref[...])


def _ffn_kernel(*refs, gated):
    if gated:
        h_ref, o_ref, sg_ref, wo_ref, g_ref, wg_ref, wu_ref, wd_ref, out_ref = refs
    else:
        h_ref, o_ref, wo_ref, g_ref, wg_ref, wu_ref, wd_ref, out_ref = refs
        sg_ref = None
    x = _mixer_residual(h_ref, o_ref, sg_ref, wo_ref)
    xn = _rms_rows(x, g_ref[...], D_MODEL).astype(BF16)
    acc = x
    for c in range(D_FF // FF_CHUNK):
        cs = slice(c * FF_CHUNK, (c + 1) * FF_CHUNK)
        g = _dot(xn, wg_ref[:, cs])
        u = _dot(xn, wu_ref[:, cs])
        act = (g * jax.nn.sigmoid(g) * u).astype(BF16)
        acc = acc + _dot(act, wd_ref[cs, :])
    out_ref[...] = acc


def _outproj_ffn(h, o, sg, wo, ffn_norm, wg, wu, wd):
    nt = h.shape[0]
    tm = TOKEN_TILE
    row = pl.BlockSpec((tm, D_MODEL), lambda t: (t, 0))
    res = lambda a: _resident(a.shape)
    gated = sg is not None
    acts = [h, o] + ([sg] if gated else [])
    weights = [wo, ffn_norm, wg, wu, wd]
    return pl.pallas_call(
        functools.partial(_ffn_kernel, gated=gated),
        grid=(nt // tm,),
        in_specs=[row] * len(acts) + [res(w) for w in weights],
        out_specs=row,
        out_shape=jax.ShapeDtypeStruct((nt, D_MODEL), F32),
        compiler_params=_cparams(("arbitrary",)),
        name="outproj_ffn",
    )(*acts, *weights)


def _router_kernel(*refs, gated):
    if gated:
        h_ref, o_ref, sg_ref, wo_ref, g_ref, wrh_ref, wrl_ref, h_out, xn_out, comb_out = refs
    else:
        h_ref, o_ref, wo_ref, g_ref, wrh_ref, wrl_ref, h_out, xn_out, comb_out = refs
        sg_ref = None
    x = _mixer_residual(h_ref, o_ref, sg_ref, wo_ref)
    h_out[...] = x
    xn = _rms_rows(x, g_ref[...], D_MODEL)
    xh = xn.astype(BF16)
    xl = (xn - xh.astype(F32)).astype(BF16)
    xn_out[...] = xn
    logits = _dot(xh, wrh_ref[...]) + _dot(xl, wrh_ref[...]) + _dot(xh, wrl_ref[...])
    lane = _lane_iota(logits.shape)
    lane_f = lane.astype(F32)
    logits = jnp.where(lane < N_EXPERTS, logits, NEG_BIG)
    t1 = jnp.max(logits, axis=-1, keepdims=True)
    i1 = jnp.min(jnp.where(logits == t1, lane_f, float(LANES)), axis=-1, keepdims=True)
    rest = jnp.where(lane_f == i1, NEG_BIG, logits)
    t2 = jnp.max(rest, axis=-1, keepdims=True)
    i2 = jnp.min(jnp.where(rest == t2, lane_f, float(LANES)), axis=-1, keepdims=True)
    e = jnp.exp(t2 - t1)
    w1 = 1.0 / (1.0 + e)
    w2 = e / (1.0 + e)
    comb_out[...] = jnp.where(lane == 0, i1, jnp.where(lane == 1, i2, jnp.where(lane == 2, w1,
                    jnp.where(lane == 3, w2, 0.0))))


def _outproj_router(h, o, sg, wo, ffn_norm, wr_hi, wr_lo):
    nt = h.shape[0]
    tm = TOKEN_TILE
    row = lambda w: pl.BlockSpec((tm, w), lambda t: (t, 0))
    res = lambda a: _resident(a.shape)
    gated = sg is not None
    acts = [h, o] + ([sg] if gated else [])
    weights = [wo, ffn_norm, wr_hi, wr_lo]
    return pl.pallas_call(
        functools.partial(_router_kernel, gated=gated),
        grid=(nt // tm,),
        in_specs=[row(D_MODEL)] * len(acts) + [res(w) for w in weights],
        out_specs=[row(D_MODEL), row(D_MODEL), row(LANES)],
        out_shape=[jax.ShapeDtypeStruct((nt, D_MODEL), F32), jax.ShapeDtypeStruct((nt, D_MODEL), F32),
                   jax.ShapeDtypeStruct((nt, LANES), F32)],
        compiler_params=_cparams(("arbitrary",)),
        name="outproj_router",
    )(*acts, *weights)


def _row_copy(src_hbm, src_row, dst, dst_row, sem):
    return pltpu.make_async_copy(src_hbm.at[pl.ds(src_row, 1)], dst.at[pl.ds(dst_row, 1)], sem)


def _drain_rows(src_hbm, dst, sem, n):
    def wait(r, c):
        _row_copy(src_hbm, 0, dst, 0, sem).wait()
        return c
    lax.fori_loop(0, n, wait, 0, unroll=8)


def _dispatch_kernel(dest_ref, zstart_ref, xn_hbm, xs_hbm, zbuf, sem, *, rows):
    t = pl.program_id(0)

    @pl.when(t == 0)
    def _():
        zbuf[...] = jnp.zeros_like(zbuf)
        for e in range(2 * N_EXPERTS):
            @pl.when(zstart_ref[e] >= 0)
            def _():
                start = pl.multiple_of(zstart_ref[e], GROUP_TILE)
                cp = pltpu.make_async_copy(zbuf, xs_hbm.at[pl.ds(start, GROUP_TILE)], sem)
                cp.start()
                cp.wait()

    base = t * rows

    def issue(r, c):
        tok = base + r
        _row_copy(xn_hbm, tok, xs_hbm, dest_ref[2 * tok], sem).start()
        _row_copy(xn_hbm, tok, xs_hbm, dest_ref[2 * tok + 1], sem).start()
        return c

    lax.fori_loop(0, rows, issue, 0, unroll=8)
    _drain_rows(xn_hbm, xs_hbm, sem, 2 * rows)


def _dispatch(dest, zstart, xn, n_rows):
    nt = xn.shape[0]
    rows = ROUTE_ROWS
    return pl.pallas_call(
        functools.partial(_dispatch_kernel, rows=rows),
        grid_spec=pltpu.PrefetchScalarGridSpec(
            num_scalar_prefetch=2, grid=(nt // rows,),
            in_specs=[pl.BlockSpec(memory_space=pl.ANY)],
            out_specs=pl.BlockSpec(memory_space=pl.ANY),
            scratch_shapes=[pltpu.VMEM((GROUP_TILE, D_MODEL), F32), pltpu.SemaphoreType.DMA(())]),
        out_shape=jax.ShapeDtypeStruct((n_rows, D_MODEL), F32),
        compiler_params=_cparams(("arbitrary",)),
        name="moe_dispatch",
    )(dest, zstart, xn)


def _grouped_ffn_kernel(te_ref, ntiles_ref, xs_ref, wg_ref, wu_ref, wd_ref, ys_ref):
    used = pl.program_id(0) < ntiles_ref[0]

    @pl.when(jnp.logical_not(used))
    def _():
        ys_ref[...] = jnp.zeros_like(ys_ref)

    @pl.when(used)
    def _():
        x = xs_ref[...].astype(BF16)
        acc = None
        for c in range(D_FF // FF_CHUNK):
            cs = slice(c * FF_CHUNK, (c + 1) * FF_CHUNK)
            g = _dot(x, wg_ref[:, cs])
            u = _dot(x, wu_ref[:, cs])
            act = (g * jax.nn.sigmoid(g) * u).astype(BF16)
            part = _dot(act, wd_ref[cs, :])
            acc = part if acc is None else acc + part
        ys_ref[...] = acc


def _grouped_ffn(tile_expert, n_tiles, xs, wg, wu, wd):
    n_rows = xs.shape[0]
    tg = GROUP_TILE
    rows = pl.BlockSpec((tg, D_MODEL), lambda i, te, n: (i, 0))
    expert = lambda a: pl.BlockSpec((None,) + a.shape[1:], lambda i, te, n: (te[i], 0, 0),
                                    pipeline_mode=pl.Buffered(1))
    return pl.pallas_call(
        _grouped_ffn_kernel,
        grid_spec=pltpu.PrefetchScalarGridSpec(
            num_scalar_prefetch=2, grid=(n_rows // tg,),
            in_specs=[rows, expert(wg), expert(wu), expert(wd)],
            out_specs=rows),
        out_shape=jax.ShapeDtypeStruct((n_rows, D_MODEL), F32),
        compiler_params=_cparams(("arbitrary",)),
        name="moe_grouped_ffn",
    )(tile_expert, n_tiles, xs, wg, wu, wd)


def _combine_kernel(dest_ref, h_ref, route_ref, ys_hbm, out_ref, buf, sem, *, rows):
    base = pl.program_id(0) * rows

    def issue(r, c):
        tok = base + r
        _row_copy(ys_hbm, dest_ref[2 * tok], buf.at[0], r, sem).start()
        _row_copy(ys_hbm, dest_ref[2 * tok + 1], buf.at[1], r, sem).start()
        return c

    lax.fori_loop(0, rows, issue, 0, unroll=8)
    _drain_rows(ys_hbm, buf.at[0], sem, 2 * rows)
    route = route_ref[...]
    lane = _lane_iota(route.shape)
    w1 = jnp.sum(jnp.where(lane == 2, route, 0.0), axis=-1, keepdims=True)
    w2 = jnp.sum(jnp.where(lane == 3, route, 0.0), axis=-1, keepdims=True)
    out_ref[...] = h_ref[...] + w1 * buf[0] + w2 * buf[1]


def _combine(dest, h, route, ys, n_out):
    nt = n_out
    rows = ROUTE_ROWS
    row = lambda w: pl.BlockSpec((rows, w), lambda t, d: (t, 0))
    return pl.pallas_call(
        functools.partial(_combine_kernel, rows=rows),
        grid_spec=pltpu.PrefetchScalarGridSpec(
            num_scalar_prefetch=1, grid=(nt // rows,),
            in_specs=[row(D_MODEL), row(LANES), pl.BlockSpec(memory_space=pl.ANY)],
            out_specs=row(D_MODEL),
            scratch_shapes=[pltpu.VMEM((2, rows, D_MODEL), F32), pltpu.SemaphoreType.DMA(())]),
        out_shape=jax.ShapeDtypeStruct((nt, D_MODEL), F32),
        compiler_params=_cparams(("arbitrary",)),
        name="moe_combine",
    )(dest, h, route, ys)


def _route_plan(route):
    tg = GROUP_TILE
    n_slots = 2 * route.shape[0]
    max_tiles = n_slots // tg + N_EXPERTS
    es = route[:, :2].astype(jnp.int32).reshape(n_slots)
    onehot = (es[:, None] == jnp.arange(N_EXPERTS, dtype=jnp.int32)[None, :]).astype(jnp.int32)
    csum = jnp.cumsum(onehot, axis=0)
    counts = csum[-1]
    padded = (counts + tg - 1) // tg * tg
    gend = jnp.cumsum(padded)
    dest = jnp.sum(onehot * (csum - 1 + (gend - padded)[None, :]), axis=1)
    n_tiles = gend[-1] // tg
    tile_row = jnp.minimum(jnp.arange(max_tiles, dtype=jnp.int32), n_tiles - 1) * tg
    tile_expert = jnp.sum((tile_row[:, None] >= gend[None, :]).astype(jnp.int32), axis=1)
    tail = (n_tiles + jnp.arange(N_EXPERTS, dtype=jnp.int32)) * tg
    zstart = jnp.concatenate([jnp.where(padded > 0, gend - tg, -1), jnp.where(tail < max_tiles * tg, tail, -1)])
    return dest, zstart, tile_expert, n_tiles.reshape(1), max_tiles * tg


def _pad_heads(w, width):
    k = w.shape[0]
    w = w.reshape(k, N_HEADS, width)
    return jnp.pad(w, ((0, 0), (0, 0), (0, LANES - width))).reshape(k, N_HEADS * LANES)


def _row128(v, offset=0):
    return jnp.pad(v.astype(F32), (offset, LANES - offset - v.shape[0])).reshape(1, LANES)


def _fox_params(mix_norm, w_in, b_f, q_norm, k_norm, w_out):
    d = D_MODEL
    w_all = jnp.concatenate([
        _pad_heads(w_in[:, 0:d], HEAD_DIM), _pad_heads(w_in[:, d:2 * d], HEAD_DIM),
        _pad_heads(w_in[:, 2 * d:3 * d], HEAD_DIM), w_in[:, 3 * d:4 * d],
        jnp.pad(w_in[:, 4 * d:], ((0, 0), (0, LANES - N_HEADS)))], axis=1).astype(BF16)
    scale = HEAD_DIM ** -0.5 * LOG2E
    return (mix_norm.reshape(1, d), w_all, _row128(b_f), _row128(q_norm * scale), _row128(k_norm),
            w_out.astype(BF16))


def _mla_params(mix_norm, w_in, q_a_norm, w_q_up, kv_a_norm, w_kv_up, q_norm, k_nope_norm, k_rope_norm, w_out):
    d = D_MODEL
    lat = MLA_Q_RANK + MLA_KV_RANK
    w_in_p = jnp.concatenate([w_in[:, :lat], jnp.zeros((d, MLA_NOPE), F32), w_in[:, lat:],
                              jnp.zeros((d, LANES - MLA_NOPE - MLA_ROPE), F32)], axis=1).astype(BF16)
    scale = (MLA_NOPE + MLA_ROPE) ** -0.5 * LOG2E
    return (mix_norm.reshape(1, d), w_in_p, q_a_norm.reshape(1, -1), kv_a_norm.reshape(1, -1),
            _pad_heads(w_q_up, MLA_NOPE + MLA_ROPE).astype(BF16), w_kv_up.astype(BF16),
            _row128(q_norm * scale), _row128(k_nope_norm), _row128(k_rope_norm, MLA_NOPE),
            w_out.astype(BF16))


def _router_split(w_router):
    w = jnp.pad(w_router, ((0, 0), (0, LANES - N_EXPERTS)))
    hi = w.astype(BF16)
    return hi, (w - hi.astype(F32)).astype(BF16)


def kernel(x, positions, meta_tokens, l0_mix_norm, l0_fox_w_in, l0_fox_b_f, l0_fox_q_norm, l0_fox_k_norm, l0_fox_w_out, l0_ffn_norm, l0_ffn_w_in, l0_ffn_w_out, l1_mix_norm, l1_mla_w_in, l1_mla_q_a_norm, l1_mla_w_q_up, l1_mla_kv_a_norm, l1_mla_w_kv_up, l1_mla_q_norm, l1_mla_k_nope_norm, l1_mla_k_rope_norm, l1_mla_w_out, l1_ffn_norm, l1_moe_w_router, l1_moe_w_in, l1_moe_w_out, l2_mix_norm, l2_fox_w_in, l2_fox_b_f, l2_fox_q_norm, l2_fox_k_norm, l2_fox_w_out, l2_ffn_norm, l2_ffn_w_in, l2_ffn_w_out, l3_mix_norm, l3_mla_w_in, l3_mla_q_a_norm, l3_mla_w_q_up, l3_mla_kv_a_norm, l3_mla_w_kv_up, l3_mla_q_norm, l3_mla_k_nope_norm, l3_mla_k_rope_norm, l3_mla_w_out, l3_ffn_norm, l3_moe_w_router, l3_moe_w_in, l3_moe_w_out):
    batch, seq, d = x.shape
    assert d == D_MODEL and seq % Q_TILE == 0 and seq % TOKEN_TILE == 0 and META_PAD % TOKEN_TILE == 0
    n_real = batch * seq
    pad_rows = META_PAD - N_META

    h = jnp.concatenate([x.reshape(n_real, d), meta_tokens.astype(x.dtype), jnp.zeros((pad_rows, d), x.dtype)], axis=0)
    pos = jnp.concatenate([positions.reshape(n_real), jnp.arange(N_META, dtype=positions.dtype),
                           jnp.zeros((pad_rows,), positions.dtype)]).astype(F32)
    pos_b = jnp.broadcast_to(pos[:, None], (n_real + META_PAD, LANES))
    inv_freq = ROPE_THETA ** (-jnp.arange(0, MLA_ROPE, 2, dtype=F32) / MLA_ROPE)
    half = MLA_ROPE // 2
    freq_p = _row128(jnp.concatenate([inv_freq, inv_freq]), MLA_NOPE)
    sign_p = _row128(jnp.concatenate([-jnp.ones((half,), F32), jnp.ones((half,), F32)]), MLA_NOPE)
    cos_p, sin_p = _rope_tables(pos_b, freq_p, sign_p)

    def fox(h, params):
        mix_norm, w_all, b_f, qg, kg, w_out = _fox_params(*params)
        q2, k2, v2, sg = _fox_pre(h, mix_norm, w_all, b_f, qg, kg, seq=seq)
        return _attention(q2, k2, v2, batch=batch, seq=seq), sg, w_out

    def mla(h, params):
        *pre, w_out = _mla_params(*params)
        q2, k2, v2 = _mla_pre(h, *pre, cos_p, sin_p)
        return _attention(q2, k2, v2, batch=batch, seq=seq), None, w_out

    def dense(h, o, sg, wo, ffn_norm, w_in, w_out):
        return _outproj_ffn(h, o, sg, wo, ffn_norm.reshape(1, d), w_in[:, :D_FF].astype(BF16),
                            w_in[:, D_FF:].astype(BF16), w_out.astype(BF16))

    def moe(h, o, sg, wo, ffn_norm, w_router, w_in, w_out, n_out):
        wr_hi, wr_lo = _router_split(w_router)
        h, xn, route = _outproj_router(h, o, sg, wo, ffn_norm.reshape(1, d), wr_hi, wr_lo)
        dest, zstart, tile_expert, n_tiles, n_rows = _route_plan(route)
        xs = _dispatch(dest, zstart, xn, n_rows)
        ys = _grouped_ffn(tile_expert, n_tiles, xs, w_in[:, :, :D_FF].astype(BF16), w_in[:, :, D_FF:].astype(BF16),
                          w_out.astype(BF16))
        return _combine(dest, h, route, ys, n_out)

    o, sg, wo = fox(h, (l0_mix_norm, l0_fox_w_in, l0_fox_b_f, l0_fox_q_norm, l0_fox_k_norm, l0_fox_w_out))
    h = dense(h, o, sg, wo, l0_ffn_norm, l0_ffn_w_in, l0_ffn_w_out)
    o, sg, wo = mla(h, (l1_mix_norm, l1_mla_w_in, l1_mla_q_a_norm, l1_mla_w_q_up, l1_mla_kv_a_norm, l1_mla_w_kv_up,
                        l1_mla_q_norm, l1_mla_k_nope_norm, l1_mla_k_rope_norm, l1_mla_w_out))
    h = moe(h, o, sg, wo, l1_ffn_norm, l1_moe_w_router, l1_moe_w_in, l1_moe_w_out, n_real + META_PAD)
    o, sg, wo = fox(h, (l2_mix_norm, l2_fox_w_in, l2_fox_b_f, l2_fox_q_norm, l2_fox_k_norm, l2_fox_w_out))
    h = dense(h, o, sg, wo, l2_ffn_norm, l2_ffn_w_in, l2_ffn_w_out)
    o, sg, wo = mla(h, (l3_mix_norm, l3_mla_w_in, l3_mla_q_a_norm, l3_mla_w_q_up, l3_mla_kv_a_norm, l3_mla_w_kv_up,
                        l3_mla_q_norm, l3_mla_k_nope_norm, l3_mla_k_rope_norm, l3_mla_w_out))
    out = moe(h, o, sg, wo, l3_ffn_norm, l3_moe_w_router, l3_moe_w_in, l3_moe_w_out, n_real)
    return out.reshape(batch, seq, d)
```

```python
import functools
import math

import jax
import jax.numpy as jnp
from jax import lax
from jax.experimental import pallas as pl
from jax.experimental.pallas import tpu as pltpu

D_MODEL = 1024
N_META = 16
EPS = 1e-6
N_HEADS = 16
HEAD_DIM = 64
MLA_NOPE = 64
MLA_ROPE = 32
MLA_Q_RANK = 256
MLA_KV_RANK = 128
ROPE_THETA = 10000.0
D_FF = 2816
N_EXPERTS = 8

LANES = 128
META_PAD = 512
TOKEN_TILE = 512
Q_TILE = 512
KV_TILE = 512
FF_CHUNK = 1408
GROUP_TILE = 512
ROUTE_ROWS = 256
LOG2E = 1.4426950408889634
NEG_BIG = -1e30
VMEM_LIMIT = 56 * 1024 * 1024

F32 = jnp.float32
BF16 = jnp.bfloat16


def _cparams(sem):
    return pltpu.CompilerParams(dimension_semantics=sem, vmem_limit_bytes=VMEM_LIMIT)


def _resident(shape):
    nd = len(shape)
    return pl.BlockSpec(shape, lambda *_: (0,) * nd, pipeline_mode=pl.Buffered(1))


def _rms_rows(x, gain, count):
    ms = jnp.sum(x * x, axis=-1, keepdims=True) * (1.0 / count)
    return x * lax.rsqrt(ms + EPS) * gain


def _lane_iota(shape):
    return lax.broadcasted_iota(jnp.int32, shape, len(shape) - 1)


def _dot(a, b):
    return jnp.dot(a, b, preferred_element_type=F32)


def _dot_nt(a, b):
    return lax.dot_general(a, b, (((1,), (1,)), ((), ())), preferred_element_type=F32)


def _split3(x):
    a = x.astype(BF16).astype(F32)
    b = (x - a).astype(BF16).astype(F32)
    c = (x - a - b).astype(BF16).astype(F32)
    return a, b, c


def _fox_pre_kernel(h_ref, g_ref, w_ref, bf_ref, qg_ref, kg_ref,
                    q_out, k_out, v_out, sg_out, carry_ref, *, tiles_per_seq, n_real_tiles):
    t = pl.program_id(0)
    tm = h_ref.shape[0]
    hp = N_HEADS * LANES
    xn = _rms_rows(h_ref[...], g_ref[...], D_MODEL).astype(BF16)

    f = _dot(xn, w_ref[:, 3 * hp + D_MODEL:]) + bf_ref[...]
    ls = jnp.minimum(f, 0.0) - jnp.log(1.0 + jnp.exp(-jnp.abs(f)))
    row = lax.broadcasted_iota(jnp.int32, (tm, LANES), 0)
    c = ls
    d = 1
    while d < tm:
        c = c + jnp.where(row >= d, pltpu.roll(c, d, axis=0), 0.0)
        d *= 2
    is_meta = t == n_real_tiles

    @pl.when(jnp.logical_or(t % tiles_per_seq == 0, is_meta))
    def _():
        carry_ref[...] = jnp.zeros_like(carry_ref)

    c = c + carry_ref[...]
    carry_ref[...] = c[tm - 1:tm, :]
    c_last_meta = jnp.sum(jnp.where(row == N_META - 1, c, 0.0), axis=0, keepdims=True)
    c = jnp.where(is_meta, c - c_last_meta, c) * LOG2E
    row1 = lax.broadcasted_iota(jnp.int32, (tm, 1), 0)
    row_valid = jnp.logical_or(jnp.logical_not(is_meta), row1 < N_META)

    lane = _lane_iota((1, LANES))
    feat = lane < HEAD_DIM
    qg = qg_ref[...]
    kg = kg_ref[...]
    pair = 2 * LANES
    for h in range(N_HEADS):
        sl = slice(h * LANES, (h + 1) * LANES)
        if h % 2 == 0:
            ps = slice(h * LANES, h * LANES + pair)
            q = _dot(xn, w_ref[:, ps])
            k = _dot(xn, w_ref[:, hp + h * LANES:hp + h * LANES + pair])
            v = _dot(xn, w_ref[:, 2 * hp + h * LANES:2 * hp + h * LANES + pair])
        hs = slice((h % 2) * LANES, (h % 2 + 1) * LANES)
        ch = jnp.sum(jnp.where(lane == h, c, 0.0), axis=-1, keepdims=True)
        c1, c2, c3 = _split3(ch)
        qh = _rms_rows(q[:, hs], qg, HEAD_DIM)
        kh = _rms_rows(k[:, hs], kg, HEAD_DIM)
        q_extra = jnp.where(lane == 64, c1, jnp.where(lane == 65, c2, jnp.where(lane == 66, c3,
                  jnp.where(lane < 70, 1.0, 0.0))))
        k_extra = jnp.where(lane < 67, 1.0,
                  jnp.where(lane == 67, jnp.where(row_valid, -c1, NEG_BIG),
                  jnp.where(lane == 68, -c2, jnp.where(lane == 69, -c3, 0.0))))
        q_out[:, sl] = jnp.where(feat, qh, q_extra).astype(BF16)
        k_out[:, sl] = jnp.where(feat, kh, k_extra).astype(BF16)
        v_out[:, sl] = jnp.where(lane == HEAD_DIM, 1.0, v[:, hs]).astype(BF16)
    gate = _dot(xn, w_ref[:, 3 * hp:3 * hp + D_MODEL])
    sg_out[...] = jax.nn.sigmoid(gate).astype(BF16)


def _fox_pre(h, mix_norm, w_all, b_f, qg, kg, *, seq):
    nt = h.shape[0]
    tm = TOKEN_TILE
    hp = N_HEADS * LANES
    n_real_tiles = (nt - META_PAD) // tm
    row = lambda w: pl.BlockSpec((tm, w), lambda t: (t, 0))
    return pl.pallas_call(
        functools.partial(_fox_pre_kernel, tiles_per_seq=seq // tm, n_real_tiles=n_real_tiles),
        grid=(nt // tm,),
        in_specs=[row(D_MODEL), _resident((1, D_MODEL)), _resident(w_all.shape),
                  _resident((1, LANES)), _resident((1, LANES)), _resident((1, LANES))],
        out_specs=[row(hp), row(hp), row(hp), row(D_MODEL)],
        out_shape=[jax.ShapeDtypeStruct((nt, hp), BF16)] * 3 + [jax.ShapeDtypeStruct((nt, D_MODEL), BF16)],
        scratch_shapes=[pltpu.VMEM((1, LANES), F32)],
        compiler_params=_cparams(("arbitrary",)),
        name="fox_pre",
    )(h, mix_norm, w_all, b_f, qg, kg)


def _rope_tables_kernel(pos_ref, freq_ref, sign_ref, cos_out, sin_out):
    ang = pos_ref[...] * freq_ref[...]
    cos_out[...] = jnp.cos(ang)
    sin_out[...] = jnp.sin(ang) * sign_ref[...]


def _rope_tables(pos_b, freq_p, sign_p):
    nt = pos_b.shape[0]
    tm = TOKEN_TILE
    row = pl.BlockSpec((tm, LANES), lambda t: (t, 0))
    return pl.pallas_call(
        _rope_tables_kernel,
        grid=(nt // tm,),
        in_specs=[row, _resident((1, LANES)), _resident((1, LANES))],
        out_specs=[row, row],
        out_shape=[jax.ShapeDtypeStruct((nt, LANES), F32)] * 2,
        compiler_params=_cparams(("arbitrary",)),
        name="rope_tables",
    )(pos_b, freq_p, sign_p)


def _rotate(x, cos_p, sin_p, lane):
    first_half = jnp.logical_and(lane >= MLA_NOPE, lane < MLA_NOPE + MLA_ROPE // 2)
    partner = jnp.where(first_half, pltpu.roll(x, LANES - MLA_ROPE // 2, axis=1),
                        pltpu.roll(x, MLA_ROPE // 2, axis=1))
    return x * cos_p + partner * sin_p


def _mla_pre_kernel(h_ref, g_ref, win_ref, qag_ref, kvag_ref, wq_ref, wkv_ref,
                    qg_ref, kng_ref, krg_ref, cos_ref, sin_ref,
                    q_out, k_out, v_out, *, n_real_tiles):
    t = pl.program_id(0)
    tm = h_ref.shape[0]
    xn = _rms_rows(h_ref[...], g_ref[...], D_MODEL).astype(BF16)
    proj = _dot(xn, win_ref[...])
    c_q = _rms_rows(proj[:, :MLA_Q_RANK], qag_ref[...], MLA_Q_RANK).astype(BF16)
    c_kv = _rms_rows(proj[:, MLA_Q_RANK:MLA_Q_RANK + MLA_KV_RANK], kvag_ref[...], MLA_KV_RANK).astype(BF16)
    lane = _lane_iota((1, LANES))
    cos_p = cos_ref[...]
    sin_p = sin_ref[...]
    row = lax.broadcasted_iota(jnp.int32, (tm, 1), 0)
    row_valid = jnp.logical_or(t != n_real_tiles, row < N_META)

    kr = _rms_rows(proj[:, MLA_Q_RANK + MLA_KV_RANK:], krg_ref[...], MLA_ROPE)
    kr = _rotate(kr, cos_p, sin_p, lane)
    kr = jnp.where(lane == MLA_NOPE + MLA_ROPE, jnp.where(row_valid, 0.0, NEG_BIG), kr)

    qg = qg_ref[...]
    kng = kng_ref[...]
    nope = lane < MLA_NOPE
    pair = 2 * LANES
    for h in range(N_HEADS):
        sl = slice(h * LANES, (h + 1) * LANES)
        if h % 2 == 0:
            q = _dot(c_q, wq_ref[:, h * LANES:h * LANES + pair])
            kv = _dot(c_kv, wkv_ref[:, h * LANES:h * LANES + pair])
        hs = slice((h % 2) * LANES, (h % 2 + 1) * LANES)
        qh = _rms_rows(q[:, hs], qg, MLA_NOPE + MLA_ROPE)
        qh = _rotate(qh, cos_p, sin_p, lane)
        q_out[:, sl] = jnp.where(lane == MLA_NOPE + MLA_ROPE, 1.0, qh).astype(BF16)
        kvh = kv[:, hs]
        kn = _rms_rows(jnp.where(nope, kvh, 0.0), kng, MLA_NOPE)
        k_out[:, sl] = jnp.where(nope, kn, kr).astype(BF16)
        vh = pltpu.roll(kvh, LANES // 2, axis=1)
        v_out[:, sl] = jnp.where(nope, vh, jnp.where(lane == MLA_NOPE, 1.0, 0.0)).astype(BF16)


def _mla_pre(h, mix_norm, w_in, qag, kvag, wq, wkv, qg, kng, krg, cos_p, sin_p):
    nt = h.shape[0]
    tm = TOKEN_TILE
    hp = N_HEADS * LANES
    row = lambda w: pl.BlockSpec((tm, w), lambda t: (t, 0))
    res = lambda a: _resident(a.shape)
    return pl.pallas_call(
        functools.partial(_mla_pre_kernel, n_real_tiles=(nt - META_PAD) // tm),
        grid=(nt // tm,),
        in_specs=[row(D_MODEL), res(mix_norm), res(w_in), res(qag), res(kvag), res(wq), res(wkv),
                  res(qg), res(kng), res(krg), row(LANES), row(LANES)],
        out_specs=[row(hp), row(hp), row(hp)],
        out_shape=[jax.ShapeDtypeStruct((nt, hp), BF16)] * 3,
        compiler_params=_cparams(("arbitrary",)),
        name="mla_pre",
    )(h, mix_norm, w_in, qag, kvag, wq, wkv, qg, kng, krg, cos_p, sin_p)


def _attn_kernel(*refs, tq, tk, batch):
    meta_queries = pl.program_id(0) == batch
    i = pl.program_id(2)

    @pl.when(jnp.logical_or(jnp.logical_not(meta_queries), i == 0))
    def _():
        _attn_block(*refs, tq=tq, tk=tk, i=i, meta_queries=meta_queries)


def _attn_block(q_ref, k_ref, v_ref, km_ref, vm_ref, o_ref, m_sc, acc_sc, s_sc, *, tq, tk, i, meta_queries):
    lane = _lane_iota((1, LANES))
    heads = [slice(a * LANES, (a + 1) * LANES) for a in range(2)]
    q = [q_ref[:, sl] for sl in heads]

    for a, sl in enumerate(heads):
        s = _dot_nt(q[a], km_ref[:, sl])
        r = lax.broadcasted_iota(jnp.int32, s.shape, 0)
        cidx = lax.broadcasted_iota(jnp.int32, s.shape, 1)
        s = jnp.where(jnp.logical_or(jnp.logical_not(meta_queries), cidx <= r), s, NEG_BIG)
        m = jnp.max(s, axis=-1, keepdims=True)
        p = jnp.exp2(s - m)
        m_sc[a] = jnp.broadcast_to(m, (m.shape[0], LANES))
        acc_sc[a] = _dot(p.astype(BF16), vm_ref[:, sl])

    def scores(blk, slot):
        start = pl.multiple_of(blk * tk, tk)
        for a, sl in enumerate(heads):
            s_sc[slot, a] = _dot_nt(q[a], k_ref[pl.ds(start, tk), sl])

    def update(blk, slot, mask=None):
        start = pl.multiple_of(blk * tk, tk)
        for a, sl in enumerate(heads):
            s = s_sc[slot, a]
            if mask is not None:
                s = jnp.where(mask, s, NEG_BIG)
            m = m_sc[a]
            m_new = jnp.maximum(m, jnp.max(s, axis=-1, keepdims=True))
            alpha = jnp.exp2(m - m_new)
            p = jnp.exp2(s - jnp.concatenate([m_new] * (tk // LANES), axis=1))
            m_sc[a] = m_new
            acc_sc[a] = alpha * acc_sc[a] + _dot(p.astype(BF16), v_ref[pl.ds(start, tk), sl])

    @pl.when(jnp.logical_not(meta_queries))
    def _():
        assert tq == tk
        scores(0, 0)

        def pair(p, _):
            scores(2 * p + 1, 1)
            update(2 * p, 0)
            scores(2 * p + 2, 0)
            update(2 * p + 1, 1)
            return 0

        lax.fori_loop(0, i // 2, pair, 0)
        r = lax.broadcasted_iota(jnp.int32, (tq, tk), 0)
        cidx = lax.broadcasted_iota(jnp.int32, (tq, tk), 1)

        @pl.when(i % 2 == 1)
        def _():
            scores(i, 1)
            update(i - 1, 0)
            update(i, 1, mask=cidx <= r)

        @pl.when(i % 2 == 0)
        def _():
            update(i, 0, mask=cidx <= r)
    outs = []
    for a in range(2):
        acc = acc_sc[a]
        denom = jnp.sum(jnp.where(lane == HEAD_DIM, acc, 0.0), axis=-1, keepdims=True)
        outs.append(acc / denom)
    packed = jnp.where(lane < HEAD_DIM, outs[0], pltpu.roll(outs[1], HEAD_DIM, axis=1))
    o_ref[...] = packed.astype(BF16)


def _attention(q2, k2, v2, *, batch, seq):
    nt = q2.shape[0]
    n_real = batch * seq
    tq, tk = Q_TILE, KV_TILE
    nq = seq // tq
    pair = 2 * LANES
    meta_blk = n_real // LANES
    assert tq == META_PAD
    q_blk = lambda b, hp, i: (jnp.where(b == batch, batch * nq, b * nq + i), hp)
    kv_blk = lambda b, hp, i: (jnp.minimum(b, batch - 1), hp)
    return pl.pallas_call(
        functools.partial(_attn_kernel, tq=tq, tk=tk, batch=batch),
        grid=(batch + 1, N_HEADS // 2, nq),
        in_specs=[pl.BlockSpec((tq, pair), q_blk),
                  pl.BlockSpec((seq, pair), kv_blk),
                  pl.BlockSpec((seq, pair), kv_blk),
                  pl.BlockSpec((LANES, pair), lambda b, hp, i: (meta_blk, hp)),
                  pl.BlockSpec((LANES, pair), lambda b, hp, i: (meta_blk, hp))],
        out_specs=pl.BlockSpec((tq, LANES), q_blk),
        out_shape=jax.ShapeDtypeStruct((nt, D_MODEL), BF16),
        scratch_shapes=[pltpu.VMEM((2, tq, LANES), F32), pltpu.VMEM((2, tq, LANES), F32),
                        pltpu.VMEM((2, 2, tq, tk), F32)],
        compiler_params=_cparams(("arbitrary", "arbitrary", "arbitrary")),
        name="attention",
    )(q2, k2, v2, k2, v2)


def _mixer_residual(h_ref, o_ref, sg_ref, wo_ref):
    a = o_ref[...]
    if sg_ref is not None:
        a = a * sg_ref[...]
    return h_ref[...] + _dot(a, wo_ref[...])


def _ffn_kernel(*refs, gated):
    if gated:
        h_ref, o_ref, sg_ref, wo_ref, g_ref, wg_ref, wu_ref, wd_ref, out_ref = refs
    else:
        h_ref, o_ref, wo_ref, g_ref, wg_ref, wu_ref, wd_ref, out_ref = refs
        sg_ref = None
    x = _mixer_residual(h_ref, o_ref, sg_ref, wo_ref)
    xn = _rms_rows(x, g_ref[...], D_MODEL).astype(BF16)
    acc = x
    for c in range(D_FF // FF_CHUNK):
        cs = slice(c * FF_CHUNK, (c + 1) * FF_CHUNK)
        g = _dot(xn, wg_ref[:, cs])
        u = _dot(xn, wu_ref[:, cs])
        act = (g * jax.nn.sigmoid(g) * u).astype(BF16)
        acc = acc + _dot(act, wd_ref[cs, :])
    out_ref[...] = acc


def _outproj_ffn(h, o, sg, wo, ffn_norm, wg, wu, wd):
    nt = h.shape[0]
    tm = TOKEN_TILE
    row = pl.BlockSpec((tm, D_MODEL), lambda t: (t, 0))
    res = lambda a: _resident(a.shape)
    gated = sg is not None
    acts = [h, o] + ([sg] if gated else [])
    weights = [wo, ffn_norm, wg, wu, wd]
    return pl.pallas_call(
        functools.partial(_ffn_kernel, gated=gated),
        grid=(nt // tm,),
        in_specs=[row] * len(acts) + [res(w) for w in weights],
        out_specs=row,
        out_shape=jax.ShapeDtypeStruct((nt, D_MODEL), F32),
        compiler_params=_cparams(("arbitrary",)),
        name="outproj_ffn",
    )(*acts, *weights)


def _router_kernel(*refs, gated):
    if gated:
        h_ref, o_ref, sg_ref, wo_ref, g_ref, wrh_ref, wrl_ref, h_out, xn_out, comb_out = refs
    else:
        h_ref, o_ref, wo_ref, g_ref, wrh_ref, wrl_ref, h_out, xn_out, comb_out = refs
        sg_ref = None
    x = _mixer_residual(h_ref, o_ref, sg_ref, wo_ref)
    h_out[...] = x
    xn = _rms_rows(x, g_ref[...], D_MODEL)
    xh = xn.astype(BF16)
    xl = (xn - xh.astype(F32)).astype(BF16)
    xn_out[...] = xn
    logits = _dot(xh, wrh_ref[...]) + _dot(xl, wrh_ref[...]) + _dot(xh, wrl_ref[...])
    lane = _lane_iota(logits.shape)
    lane_f = lane.astype(F32)
    logits = jnp.where(lane < N_EXPERTS, logits, NEG_BIG)
    t1 = jnp.max(logits, axis=-1, keepdims=True)
    i1 = jnp.min(jnp.where(logits == t1, lane_f, float(LANES)), axis=-1, keepdims=True)
    rest = jnp.where(lane_f == i1, NEG_BIG, logits)
    t2 = jnp.max(rest, axis=-1, keepdims=True)
    i2 = jnp.min(jnp.where(rest == t2, lane_f, float(LANES)), axis=-1, keepdims=True)
    e = jnp.exp(t2 - t1)
    w1 = 1.0 / (1.0 + e)
    w2 = e / (1.0 + e)
    comb_out[...] = jnp.where(lane == 0, i1, jnp.where(lane == 1, i2, jnp.where(lane == 2, w1,
                    jnp.where(lane == 3, w2, 0.0))))


def _outproj_router(h, o, sg, wo, ffn_norm, wr_hi, wr_lo):
    nt = h.shape[0]
    tm = TOKEN_TILE
    row = lambda w: pl.BlockSpec((tm, w), lambda t: (t, 0))
    res = lambda a: _resident(a.shape)
    gated = sg is not None
    acts = [h, o] + ([sg] if gated else [])
    weights = [wo, ffn_norm, wr_hi, wr_lo]
    return pl.pallas_call(
        functools.partial(_router_kernel, gated=gated),
        grid=(nt // tm,),
        in_specs=[row(D_MODEL)] * len(acts) + [res(w) for w in weights],
        out_specs=[row(D_MODEL), row(D_MODEL), row(LANES)],
        out_shape=[jax.ShapeDtypeStruct((nt, D_MODEL), F32), jax.ShapeDtypeStruct((nt, D_MODEL), F32),
                   jax.ShapeDtypeStruct((nt, LANES), F32)],
        compiler_params=_cparams(("arbitrary",)),
        name="outproj_router",
    )(*acts, *weights)


def _row_copy(src_hbm, src_row, dst, dst_row, sem):
    return pltpu.make_async_copy(src_hbm.at[pl.ds(src_row, 1)], dst.at[pl.ds(dst_row, 1)], sem)


def _drain_rows(src_hbm, dst, sem, n):
    def wait(r, c):
        _row_copy(src_hbm, 0, dst, 0, sem).wait()
        return c
    lax.fori_loop(0, n, wait, 0, unroll=8)


def _dispatch_kernel(dest_ref, zstart_ref, xn_ref, xs_hbm, zbuf, sem, *, rows):
    t = pl.program_id(0)

    @pl.when(t == 0)
    def _():
        zbuf[...] = jnp.zeros_like(zbuf)
        for e in range(2 * N_EXPERTS):
            @pl.when(zstart_ref[e] >= 0)
            def _():
                start = pl.multiple_of(zstart_ref[e], GROUP_TILE)
                cp = pltpu.make_async_copy(zbuf, xs_hbm.at[pl.ds(start, GROUP_TILE)], sem)
                cp.start()
                cp.wait()

    base = t * rows

    def issue(r, c):
        tok = base + r
        _row_copy(xn_ref, r, xs_hbm, dest_ref[2 * tok], sem).start()
        _row_copy(xn_ref, r, xs_hbm, dest_ref[2 * tok + 1], sem).start()
        return c

    lax.fori_loop(0, rows, issue, 0, unroll=8)
    _drain_rows(xn_ref, xs_hbm, sem, 2 * rows)


def _dispatch(dest, zstart, xn, n_rows):
    nt = xn.shape[0]
    rows = ROUTE_ROWS
    return pl.pallas_call(
        functools.partial(_dispatch_kernel, rows=rows),
        grid_spec=pltpu.PrefetchScalarGridSpec(
            num_scalar_prefetch=2, grid=(nt // rows,),
            in_specs=[pl.BlockSpec((rows, D_MODEL), lambda t, dest, zstart: (t, 0))],
            out_specs=pl.BlockSpec(memory_space=pl.ANY),
            scratch_shapes=[pltpu.VMEM((GROUP_TILE, D_MODEL), F32), pltpu.SemaphoreType.DMA(())]),
        out_shape=jax.ShapeDtypeStruct((n_rows, D_MODEL), F32),
        compiler_params=_cparams(("arbitrary",)),
        name="moe_dispatch",
    )(dest, zstart, xn)


def _grouped_ffn_kernel(te_ref, ntiles_ref, xs_ref, wg_ref, wu_ref, wd_ref, ys_ref):
    used = pl.program_id(0) < ntiles_ref[0]

    @pl.when(jnp.logical_not(used))
    def _():
        ys_ref[...] = jnp.zeros_like(ys_ref)

    @pl.when(used)
    def _():
        x = xs_ref[...].astype(BF16)
        acc = None
        for c in range(D_FF // FF_CHUNK):
            cs = slice(c * FF_CHUNK, (c + 1) * FF_CHUNK)
            g = _dot(x, wg_ref[:, cs])
            u = _dot(x, wu_ref[:, cs])
            act = (g * jax.nn.sigmoid(g) * u).astype(BF16)
            part = _dot(act, wd_ref[cs, :])
            acc = part if acc is None else acc + part
        ys_ref[...] = acc


def _grouped_ffn(tile_expert, n_tiles, xs, wg, wu, wd):
    n_rows = xs.shape[0]
    tg = GROUP_TILE
    rows = pl.BlockSpec((tg, D_MODEL), lambda i, te, n: (i, 0))
    expert = lambda a: pl.BlockSpec((None,) + a.shape[1:], lambda i, te, n: (te[i], 0, 0),
                                    pipeline_mode=pl.Buffered(1))
    return pl.pallas_call(
        _grouped_ffn_kernel,
        grid_spec=pltpu.PrefetchScalarGridSpec(
            num_scalar_prefetch=2, grid=(n_rows // tg,),
            in_specs=[rows, expert(wg), expert(wu), expert(wd)],
            out_specs=rows),
        out_shape=jax.ShapeDtypeStruct((n_rows, D_MODEL), F32),
        compiler_params=_cparams(("arbitrary",)),
        name="moe_grouped_ffn",
    )(tile_expert, n_tiles, xs, wg, wu, wd)


def _combine_kernel(dest_ref, h_ref, route_ref, ys_hbm, out_ref, buf, sem, *, rows):
    base = pl.program_id(0) * rows

    def issue(r, c):
        tok = base + r
        _row_copy(ys_hbm, dest_ref[2 * tok], buf.at[0], r, sem).start()
        _row_copy(ys_hbm, dest_ref[2 * tok + 1], buf.at[1], r, sem).start()
        return c

    lax.fori_loop(0, rows, issue, 0, unroll=8)
    _drain_rows(ys_hbm, buf.at[0], sem, 2 * rows)
    route = route_ref[...]
    lane = _lane_iota(route.shape)
    w1 = jnp.sum(jnp.where(lane == 2, route, 0.0), axis=-1, keepdims=True)
    w2 = jnp.sum(jnp.where(lane == 3, route, 0.0), axis=-1, keepdims=True)
    out_ref[...] = h_ref[...] + w1 * buf[0] + w2 * buf[1]


def _combine(dest, h, route, ys, n_out):
    nt = n_out
    rows = ROUTE_ROWS
    row = lambda w: pl.BlockSpec((rows, w), lambda t, d: (t, 0))
    return pl.pallas_call(
        functools.partial(_combine_kernel, rows=rows),
        grid_spec=pltpu.PrefetchScalarGridSpec(
            num_scalar_prefetch=1, grid=(nt // rows,),
            in_specs=[row(D_MODEL), row(LANES), pl.BlockSpec(memory_space=pl.ANY)],
            out_specs=row(D_MODEL),
            scratch_shapes=[pltpu.VMEM((2, rows, D_MODEL), F32), pltpu.SemaphoreType.DMA(())]),
        out_shape=jax.ShapeDtypeStruct((nt, D_MODEL), F32),
        compiler_params=_cparams(("arbitrary",)),
        name="moe_combine",
    )(dest, h, route, ys)


def _route_plan(route):
    tg = GROUP_TILE
    n_slots = 2 * route.shape[0]
    max_tiles = n_slots // tg + N_EXPERTS
    es = route[:, :2].astype(jnp.int32).reshape(n_slots)
    onehot = (es[:, None] == jnp.arange(N_EXPERTS, dtype=jnp.int32)[None, :]).astype(jnp.int32)
    csum = jnp.cumsum(onehot, axis=0)
    counts = csum[-1]
    padded = (counts + tg - 1) // tg * tg
    gend = jnp.cumsum(padded)
    dest = jnp.sum(onehot * (csum - 1 + (gend - padded)[None, :]), axis=1)
    n_tiles = gend[-1] // tg
    tile_row = jnp.minimum(jnp.arange(max_tiles, dtype=jnp.int32), n_tiles - 1) * tg
    tile_expert = jnp.sum((tile_row[:, None] >= gend[None, :]).astype(jnp.int32), axis=1)
    tail = (n_tiles + jnp.arange(N_EXPERTS, dtype=jnp.int32)) * tg
    zstart = jnp.concatenate([jnp.where(padded > 0, gend - tg, -1), jnp.where(tail < max_tiles * tg, tail, -1)])
    return dest, zstart, tile_expert, n_tiles.reshape(1), max_tiles * tg


def _pad_heads(w, width):
    k = w.shape[0]
    w = w.reshape(k, N_HEADS, width)
    return jnp.pad(w, ((0, 0), (0, 0), (0, LANES - width))).reshape(k, N_HEADS * LANES)


def _row128(v, offset=0):
    return jnp.pad(v.astype(F32), (offset, LANES - offset - v.shape[0])).reshape(1, LANES)


def _fox_params(mix_norm, w_in, b_f, q_norm, k_norm, w_out):
    d = D_MODEL
    w_all = jnp.concatenate([
        _pad_heads(w_in[:, 0:d], HEAD_DIM), _pad_heads(w_in[:, d:2 * d], HEAD_DIM),
        _pad_heads(w_in[:, 2 * d:3 * d], HEAD_DIM), w_in[:, 3 * d:4 * d],
        jnp.pad(w_in[:, 4 * d:], ((0, 0), (0, LANES - N_HEADS)))], axis=1).astype(BF16)
    scale = HEAD_DIM ** -0.5 * LOG2E
    return (mix_norm.reshape(1, d), w_all, _row128(b_f), _row128(q_norm * scale), _row128(k_norm),
            w_out.astype(BF16))


def _mla_params(mix_norm, w_in, q_a_norm, w_q_up, kv_a_norm, w_kv_up, q_norm, k_nope_norm, k_rope_norm, w_out):
    d = D_MODEL
    lat = MLA_Q_RANK + MLA_KV_RANK
    w_in_p = jnp.concatenate([w_in[:, :lat], jnp.zeros((d, MLA_NOPE), F32), w_in[:, lat:],
                              jnp.zeros((d, LANES - MLA_NOPE - MLA_ROPE), F32)], axis=1).astype(BF16)
    scale = (MLA_NOPE + MLA_ROPE) ** -0.5 * LOG2E
    return (mix_norm.reshape(1, d), w_in_p, q_a_norm.reshape(1, -1), kv_a_norm.reshape(1, -1),
            _pad_heads(w_q_up, MLA_NOPE + MLA_ROPE).astype(BF16), w_kv_up.astype(BF16),
            _row128(q_norm * scale), _row128(k_nope_norm), _row128(k_rope_norm, MLA_NOPE),
            w_out.astype(BF16))


def _router_split(w_router):
    w = jnp.pad(w_router, ((0, 0), (0, LANES - N_EXPERTS)))
    hi = w.astype(BF16)
    return hi, (w - hi.astype(F32)).astype(BF16)


def kernel(x, positions, meta_tokens, l0_mix_norm, l0_fox_w_in, l0_fox_b_f, l0_fox_q_norm, l0_fox_k_norm, l0_fox_w_out, l0_ffn_norm, l0_ffn_w_in, l0_ffn_w_out, l1_mix_norm, l1_mla_w_in, l1_mla_q_a_norm, l1_mla_w_q_up, l1_mla_kv_a_norm, l1_mla_w_kv_up, l1_mla_q_norm, l1_mla_k_nope_norm, l1_mla_k_rope_norm, l1_mla_w_out, l1_ffn_norm, l1_moe_w_router, l1_moe_w_in, l1_moe_w_out, l2_mix_norm, l2_fox_w_in, l2_fox_b_f, l2_fox_q_norm, l2_fox_k_norm, l2_fox_w_out, l2_ffn_norm, l2_ffn_w_in, l2_ffn_w_out, l3_mix_norm, l3_mla_w_in, l3_mla_q_a_norm, l3_mla_w_q_up, l3_mla_kv_a_norm, l3_mla_w_kv_up, l3_mla_q_norm, l3_mla_k_nope_norm, l3_mla_k_rope_norm, l3_mla_w_out, l3_ffn_norm, l3_moe_w_router, l3_moe_w_in, l3_moe_w_out):
    batch, seq, d = x.shape
    assert d == D_MODEL and seq % Q_TILE == 0 and seq % TOKEN_TILE == 0 and META_PAD % TOKEN_TILE == 0
    n_real = batch * seq
    pad_rows = META_PAD - N_META

    h = jnp.concatenate([x.reshape(n_real, d), meta_tokens.astype(x.dtype), jnp.zeros((pad_rows, d), x.dtype)], axis=0)
    pos = jnp.concatenate([positions.reshape(n_real), jnp.arange(N_META, dtype=positions.dtype),
                           jnp.zeros((pad_rows,), positions.dtype)]).astype(F32)
    pos_b = jnp.broadcast_to(pos[:, None], (n_real + META_PAD, LANES))
    inv_freq = ROPE_THETA ** (-jnp.arange(0, MLA_ROPE, 2, dtype=F32) / MLA_ROPE)
    half = MLA_ROPE // 2
    freq_p = _row128(jnp.concatenate([inv_freq, inv_freq]), MLA_NOPE)
    sign_p = _row128(jnp.concatenate([-jnp.ones((half,), F32), jnp.ones((half,), F32)]), MLA_NOPE)
    cos_p, sin_p = _rope_tables(pos_b, freq_p, sign_p)

    def fox(h, params):
        mix_norm, w_all, b_f, qg, kg, w_out = _fox_params(*params)
        q2, k2, v2, sg = _fox_pre(h, mix_norm, w_all, b_f, qg, kg, seq=seq)
        return _attention(q2, k2, v2, batch=batch, seq=seq), sg, w_out

    def mla(h, params):
        *pre, w_out = _mla_params(*params)
        q2, k2, v2 = _mla_pre(h, *pre, cos_p, sin_p)
        return _attention(q2, k2, v2, batch=batch, seq=seq), None, w_out

    def dense(h, o, sg, wo, ffn_norm, w_in, w_out):
        return _outproj_ffn(h, o, sg, wo, ffn_norm.reshape(1, d), w_in[:, :D_FF].astype(BF16),
                            w_in[:, D_FF:].astype(BF16), w_out.astype(BF16))

    def moe(h, o, sg, wo, ffn_norm, w_router, w_in, w_out, n_out):
        wr_hi, wr_lo = _router_split(w_router)
        h, xn, route = _outproj_router(h, o, sg, wo, ffn_norm.reshape(1, d), wr_hi, wr_lo)
        dest, zstart, tile_expert, n_tiles, n_rows = _route_plan(route)
        xs = _dispatch(dest, zstart, xn, n_rows)
        ys = _grouped_ffn(tile_expert, n_tiles, xs, w_in[:, :, :D_FF].astype(BF16), w_in[:, :, D_FF:].astype(BF16),
                          w_out.astype(BF16))
        return _combine(dest, h, route, ys, n_out)

    o, sg, wo = fox(h, (l0_mix_norm, l0_fox_w_in, l0_fox_b_f, l0_fox_q_norm, l0_fox_k_norm, l0_fox_w_out))
    h = dense(h, o, sg, wo, l0_ffn_norm, l0_ffn_w_in, l0_ffn_w_out)
    o, sg, wo = mla(h, (l1_mix_norm, l1_mla_w_in, l1_mla_q_a_norm, l1_mla_w_q_up, l1_mla_kv_a_norm, l1_mla_w_kv_up,
                        l1_mla_q_norm, l1_mla_k_nope_norm, l1_mla_k_rope_norm, l1_mla_w_out))
    h = moe(h, o, sg, wo, l1_ffn_norm, l1_moe_w_router, l1_moe_w_in, l1_moe_w_out, n_real + META_PAD)
    o, sg, wo = fox(h, (l2_mix_norm, l2_fox_w_in, l2_fox_b_f, l2_fox_q_norm, l2_fox_k_norm, l2_fox_w_out))
    h = dense(h, o, sg, wo, l2_ffn_norm, l2_ffn_w_in, l2_ffn_w_out)
    o, sg, wo = mla(h, (l3_mix_norm, l3_mla_w_in, l3_mla_q_a_norm, l3_mla_w_q_up, l3_mla_kv_a_norm, l3_mla_w_kv_up,
                        l3_mla_q_norm, l3_mla_k_nope_norm, l3_mla_k_rope_norm, l3_mla_w_out))
    out = moe(h, o, sg, wo, l3_ffn_norm, l3_moe_w_router, l3_moe_w_in, l3_moe_w_out, n_real)
    return out.reshape(batch, seq, d)
```

```python
import functools
import math

import jax
import jax.numpy as jnp
from jax import lax
from jax.experimental import pallas as pl
from jax.experimental.pallas import tpu as pltpu

D_MODEL = 1024
N_META = 16
EPS = 1e-6
N_HEADS = 16
HEAD_DIM = 64
MLA_NOPE = 64
MLA_ROPE = 32
MLA_Q_RANK = 256
MLA_KV_RANK = 128
ROPE_THETA = 10000.0
D_FF = 2816
N_EXPERTS = 8

LANES = 128
META_PAD = 512
TOKEN_TILE = 512
Q_TILE = 512
KV_TILE = 512
FF_CHUNK = 1408
GROUP_TILE = 512
ROUTE_ROWS = 256
LOG2E = 1.4426950408889634
NEG_BIG = -1e30
VMEM_LIMIT = 56 * 1024 * 1024

F32 = jnp.float32
BF16 = jnp.bfloat16


def _cparams(sem):
    return pltpu.CompilerParams(dimension_semantics=sem, vmem_limit_bytes=VMEM_LIMIT)


def _resident(shape):
    nd = len(shape)
    return pl.BlockSpec(shape, lambda *_: (0,) * nd, pipeline_mode=pl.Buffered(1))


def _rms_rows(x, gain, count):
    ms = jnp.sum(x * x, axis=-1, keepdims=True) * (1.0 / count)
    return x * lax.rsqrt(ms + EPS) * gain


def _lane_iota(shape):
    return lax.broadcasted_iota(jnp.int32, shape, len(shape) - 1)


def _dot(a, b):
    return jnp.dot(a, b, preferred_element_type=F32)


def _dot_nt(a, b):
    return lax.dot_general(a, b, (((1,), (1,)), ((), ())), preferred_element_type=F32)


def _split3(x):
    a = x.astype(BF16).astype(F32)
    b = (x - a).astype(BF16).astype(F32)
    c = (x - a - b).astype(BF16).astype(F32)
    return a, b, c


def _fox_pre_kernel(h_ref, g_ref, w_ref, bf_ref, qg_ref, kg_ref,
                    q_out, k_out, v_out, sg_out, carry_ref, *, tiles_per_seq, n_real_tiles):
    t = pl.program_id(0)
    tm = h_ref.shape[0]
    hp = N_HEADS * LANES
    xn = _rms_rows(h_ref[...], g_ref[...], D_MODEL).astype(BF16)

    f = _dot(xn, w_ref[:, 3 * hp + D_MODEL:]) + bf_ref[...]
    ls = jnp.minimum(f, 0.0) - jnp.log(1.0 + jnp.exp(-jnp.abs(f)))
    row = lax.broadcasted_iota(jnp.int32, (tm, LANES), 0)
    c = ls
    d = 1
    while d < tm:
        c = c + jnp.where(row >= d, pltpu.roll(c, d, axis=0), 0.0)
        d *= 2
    is_meta = t == n_real_tiles

    @pl.when(jnp.logical_or(t % tiles_per_seq == 0, is_meta))
    def _():
        carry_ref[...] = jnp.zeros_like(carry_ref)

    c = c + carry_ref[...]
    carry_ref[...] = c[tm - 1:tm, :]
    c_last_meta = jnp.sum(jnp.where(row == N_META - 1, c, 0.0), axis=0, keepdims=True)
    c = jnp.where(is_meta, c - c_last_meta, c) * LOG2E
    row1 = lax.broadcasted_iota(jnp.int32, (tm, 1), 0)
    row_valid = jnp.logical_or(jnp.logical_not(is_meta), row1 < N_META)

    lane = _lane_iota((1, LANES))
    feat = lane < HEAD_DIM
    qg = qg_ref[...]
    kg = kg_ref[...]
    pair = 2 * LANES
    for h in range(N_HEADS):
        sl = slice(h * LANES, (h + 1) * LANES)
        if h % 2 == 0:
            ps = slice(h * LANES, h * LANES + pair)
            q = _dot(xn, w_ref[:, ps])
            k = _dot(xn, w_ref[:, hp + h * LANES:hp + h * LANES + pair])
            v = _dot(xn, w_ref[:, 2 * hp + h * LANES:2 * hp + h * LANES + pair])
        hs = slice((h % 2) * LANES, (h % 2 + 1) * LANES)
        ch = jnp.sum(jnp.where(lane == h, c, 0.0), axis=-1, keepdims=True)
        c1, c2, c3 = _split3(ch)
        qh = _rms_rows(q[:, hs], qg, HEAD_DIM)
        kh = _rms_rows(k[:, hs], kg, HEAD_DIM)
        q_extra = jnp.where(lane == 64, c1, jnp.where(lane == 65, c2, jnp.where(lane == 66, c3,
                  jnp.where(lane < 70, 1.0, 0.0))))
        k_extra = jnp.where(lane < 67, 1.0,
                  jnp.where(lane == 67, jnp.where(row_valid, -c1, NEG_BIG),
                  jnp.where(lane == 68, -c2, jnp.where(lane == 69, -c3, 0.0))))
        q_out[:, sl] = jnp.where(feat, qh, q_extra).astype(BF16)
        k_out[:, sl] = jnp.where(feat, kh, k_extra).astype(BF16)
        v_out[:, sl] = jnp.where(lane == HEAD_DIM, 1.0, v[:, hs]).astype(BF16)
    gate = _dot(xn, w_ref[:, 3 * hp:3 * hp + D_MODEL])
    sg_out[...] = jax.nn.sigmoid(gate).astype(BF16)


def _fox_pre(h, mix_norm, w_all, b_f, qg, kg, *, seq):
    nt = h.shape[0]
    tm = TOKEN_TILE
    hp = N_HEADS * LANES
    n_real_tiles = (nt - META_PAD) // tm
    row = lambda w: pl.BlockSpec((tm, w), lambda t: (t, 0))
    return pl.pallas_call(
        functools.partial(_fox_pre_kernel, tiles_per_seq=seq // tm, n_real_tiles=n_real_tiles),
        grid=(nt // tm,),
        in_specs=[row(D_MODEL), _resident((1, D_MODEL)), _resident(w_all.shape),
                  _resident((1, LANES)), _resident((1, LANES)), _resident((1, LANES))],
        out_specs=[row(hp), row(hp), row(hp), row(D_MODEL)],
        out_shape=[jax.ShapeDtypeStruct((nt, hp), BF16)] * 3 + [jax.ShapeDtypeStruct((nt, D_MODEL), BF16)],
        scratch_shapes=[pltpu.VMEM((1, LANES), F32)],
        compiler_params=_cparams(("arbitrary",)),
        name="fox_pre",
    )(h, mix_norm, w_all, b_f, qg, kg)


def _rope_tables_kernel(pos_ref, freq_ref, sign_ref, cos_out, sin_out):
    ang = pos_ref[...] * freq_ref[...]
    cos_out[...] = jnp.cos(ang)
    sin_out[...] = jnp.sin(ang) * sign_ref[...]


def _rope_tables(pos_b, freq_p, sign_p):
    nt = pos_b.shape[0]
    tm = TOKEN_TILE
    row = pl.BlockSpec((tm, LANES), lambda t: (t, 0))
    return pl.pallas_call(
        _rope_tables_kernel,
        grid=(nt // tm,),
        in_specs=[row, _resident((1, LANES)), _resident((1, LANES))],
        out_specs=[row, row],
        out_shape=[jax.ShapeDtypeStruct((nt, LANES), F32)] * 2,
        compiler_params=_cparams(("arbitrary",)),
        name="rope_tables",
    )(pos_b, freq_p, sign_p)


def _rotate(x, cos_p, sin_p, lane):
    first_half = jnp.logical_and(lane >= MLA_NOPE, lane < MLA_NOPE + MLA_ROPE // 2)
    partner = jnp.where(first_half, pltpu.roll(x, LANES - MLA_ROPE // 2, axis=1),
                        pltpu.roll(x, MLA_ROPE // 2, axis=1))
    return x * cos_p + partner * sin_p


def _mla_pre_kernel(h_ref, g_ref, win_ref, qag_ref, kvag_ref, wq_ref, wkv_ref,
                    qg_ref, kng_ref, krg_ref, cos_ref, sin_ref,
                    q_out, k_out, v_out, *, n_real_tiles):
    t = pl.program_id(0)
    tm = h_ref.shape[0]
    xn = _rms_rows(h_ref[...], g_ref[...], D_MODEL).astype(BF16)
    proj = _dot(xn, win_ref[...])
    c_q = _rms_rows(proj[:, :MLA_Q_RANK], qag_ref[...], MLA_Q_RANK).astype(BF16)
    c_kv = _rms_rows(proj[:, MLA_Q_RANK:MLA_Q_RANK + MLA_KV_RANK], kvag_ref[...], MLA_KV_RANK).astype(BF16)
    lane = _lane_iota((1, LANES))
    cos_p = cos_ref[...]
    sin_p = sin_ref[...]
    row = lax.broadcasted_iota(jnp.int32, (tm, 1), 0)
    row_valid = jnp.logical_or(t != n_real_tiles, row < N_META)

    kr = _rms_rows(proj[:, MLA_Q_RANK + MLA_KV_RANK:], krg_ref[...], MLA_ROPE)
    kr = _rotate(kr, cos_p, sin_p, lane)
    kr = jnp.where(lane == MLA_NOPE + MLA_ROPE, jnp.where(row_valid, 0.0, NEG_BIG), kr)

    qg = qg_ref[...]
    kng = kng_ref[...]
    nope = lane < MLA_NOPE
    pair = 2 * LANES
    for h in range(N_HEADS):
        sl = slice(h * LANES, (h + 1) * LANES)
        if h % 2 == 0:
            q = _dot(c_q, wq_ref[:, h * LANES:h * LANES + pair])
            kv = _dot(c_kv, wkv_ref[:, h * LANES:h * LANES + pair])
        hs = slice((h % 2) * LANES, (h % 2 + 1) * LANES)
        qh = _rms_rows(q[:, hs], qg, MLA_NOPE + MLA_ROPE)
        qh = _rotate(qh, cos_p, sin_p, lane)
        q_out[:, sl] = jnp.where(lane == MLA_NOPE + MLA_ROPE, 1.0, qh).astype(BF16)
        kvh = kv[:, hs]
        kn = _rms_rows(jnp.where(nope, kvh, 0.0), kng, MLA_NOPE)
        k_out[:, sl] = jnp.where(nope, kn, kr).astype(BF16)
        vh = pltpu.roll(kvh, LANES // 2, axis=1)
        v_out[:, sl] = jnp.where(nope, vh, jnp.where(lane == MLA_NOPE, 1.0, 0.0)).astype(BF16)


def _mla_pre(h, mix_norm, w_in, qag, kvag, wq, wkv, qg, kng, krg, cos_p, sin_p):
    nt = h.shape[0]
    tm = TOKEN_TILE
    hp = N_HEADS * LANES
    row = lambda w: pl.BlockSpec((tm, w), lambda t: (t, 0))
    res = lambda a: _resident(a.shape)
    return pl.pallas_call(
        functools.partial(_mla_pre_kernel, n_real_tiles=(nt - META_PAD) // tm),
        grid=(nt // tm,),
        in_specs=[row(D_MODEL), res(mix_norm), res(w_in), res(qag), res(kvag), res(wq), res(wkv),
                  res(qg), res(kng), res(krg), row(LANES), row(LANES)],
        out_specs=[row(hp), row(hp), row(hp)],
        out_shape=[jax.ShapeDtypeStruct((nt, hp), BF16)] * 3,
        compiler_params=_cparams(("arbitrary",)),
        name="mla_pre",
    )(h, mix_norm, w_in, qag, kvag, wq, wkv, qg, kng, krg, cos_p, sin_p)


def _attn_kernel(*refs, tq, tk, batch):
    meta_queries = pl.program_id(0) == batch
    i = pl.program_id(2)

    @pl.when(jnp.logical_or(jnp.logical_not(meta_queries), i == 0))
    def _():
        _attn_block(*refs, tq=tq, tk=tk, i=i, meta_queries=meta_queries)


def _attn_block(q_ref, k_ref, v_ref, km_ref, vm_ref, o_ref, m_sc, acc_sc, sm_sc, s_sc, *, tq, tk, i, meta_queries):
    lane = _lane_iota((1, LANES))
    heads = [slice(a * LANES, (a + 1) * LANES) for a in range(2)]
    q = [q_ref[:, sl] for sl in heads]

    def scores(blk, slot):
        start = pl.multiple_of(blk * tk, tk)
        for a, sl in enumerate(heads):
            s_sc[slot, a] = _dot_nt(q[a], k_ref[pl.ds(start, tk), sl])

    def update(blk, slot, last=False):
        start = pl.multiple_of(blk * tk, tk)
        for a, sl in enumerate(heads):
            s = s_sc[slot, a]
            m = m_sc[a]
            if last:
                r = lax.broadcasted_iota(jnp.int32, (tq, tk), 0)
                cidx = lax.broadcasted_iota(jnp.int32, (tq, tk), 1)
                s = jnp.where(jnp.logical_and(real_queries, cidx <= r), s, NEG_BIG)
                rm = lax.broadcasted_iota(jnp.int32, (tq, LANES), 0)
                cm = lax.broadcasted_iota(jnp.int32, (tq, LANES), 1)
                sm = jnp.where(jnp.logical_or(real_queries, cm <= rm), sm_sc[a], NEG_BIG)
                m = jnp.maximum(m, jnp.max(sm, axis=-1, keepdims=True))
            m_new = jnp.maximum(m, jnp.max(s, axis=-1, keepdims=True))
            alpha = jnp.exp2(m_sc[a] - m_new)
            p = jnp.exp2(s - jnp.concatenate([m_new] * (tk // LANES), axis=1))
            pv = _dot(p.astype(BF16), v_ref[pl.ds(start, tk), sl])
            if last:
                pv = pv + _dot(jnp.exp2(sm - m_new).astype(BF16), vm_ref[:, sl])
            m_sc[a] = m_new
            acc_sc[a] = alpha * acc_sc[a] + pv

    assert tq == tk
    real_queries = jnp.logical_not(meta_queries)
    for a, sl in enumerate(heads):
        sm_sc[a] = _dot_nt(q[a], km_ref[:, sl])
        m_sc[a] = jnp.full(m_sc.shape[1:], NEG_BIG, F32)
        acc_sc[a] = jnp.zeros(acc_sc.shape[1:], F32)
    scores(0, 0)

    def pair(p):
        scores(2 * p + 1, 1)
        update(2 * p, 0)
        scores(2 * p + 2, 0)
        update(2 * p + 1, 1)

    def two_pairs(t, carry):
        pair(2 * t)
        pair(2 * t + 1)
        return carry

    lax.fori_loop(0, i // 4, two_pairs, 0)

    @pl.when((i // 2) % 2 == 1)
    def _():
        pair(2 * (i // 4))

    @pl.when(i % 2 == 1)
    def _():
        scores(i, 1)
        update(i - 1, 0)
        update(i, 1, last=True)

    @pl.when(i % 2 == 0)
    def _():
        update(i, 0, last=True)
    outs = []
    for a in range(2):
        acc = acc_sc[a]
        denom = jnp.sum(jnp.where(lane == HEAD_DIM, acc, 0.0), axis=-1, keepdims=True)
        outs.append(acc / denom)
    packed = jnp.where(lane < HEAD_DIM, outs[0], pltpu.roll(outs[1], HEAD_DIM, axis=1))
    o_ref[...] = packed.astype(BF16)


def _attention(q2, k2, v2, *, batch, seq):
    nt = q2.shape[0]
    n_real = batch * seq
    tq, tk = Q_TILE, KV_TILE
    nq = seq // tq
    pair = 2 * LANES
    meta_blk = n_real // LANES
    assert tq == META_PAD
    q_blk = lambda b, hp, i: (jnp.where(b == batch, batch * nq, b * nq + i), hp)
    kv_blk = lambda b, hp, i: (jnp.minimum(b, batch - 1), hp)
    return pl.pallas_call(
        functools.partial(_attn_kernel, tq=tq, tk=tk, batch=batch),
        grid=(batch + 1, N_HEADS // 2, nq),
        in_specs=[pl.BlockSpec((tq, pair), q_blk),
                  pl.BlockSpec((seq, pair), kv_blk),
                  pl.BlockSpec((seq, pair), kv_blk),
                  pl.BlockSpec((LANES, pair), lambda b, hp, i: (meta_blk, hp)),
                  pl.BlockSpec((LANES, pair), lambda b, hp, i: (meta_blk, hp))],
        out_specs=pl.BlockSpec((tq, LANES), q_blk),
        out_shape=jax.ShapeDtypeStruct((nt, D_MODEL), BF16),
        scratch_shapes=[pltpu.VMEM((2, tq, LANES), F32), pltpu.VMEM((2, tq, LANES), F32),
                        pltpu.VMEM((2, tq, LANES), F32), pltpu.VMEM((2, 2, tq, tk), F32)],
        compiler_params=_cparams(("arbitrary", "arbitrary", "arbitrary")),
        name="attention",
    )(q2, k2, v2, k2, v2)


def _mixer_residual(h_ref, o_ref, sg_ref, wo_ref):
    a = o_ref[...]
    if sg_ref is not None:
        a = a * sg_ref[...]
    return h_ref[...] + _dot(a, wo_ref[...])


def _ffn_kernel(*refs, gated):
    if gated:
        h_ref, o_ref, sg_ref, wo_ref, g_ref, wg_ref, wu_ref, wd_ref, out_ref = refs
    else:
        h_ref, o_ref, wo_ref, g_ref, wg_ref, wu_ref, wd_ref, out_ref = refs
        sg_ref = None
    x = _mixer_residual(h_ref, o_ref, sg_ref, wo_ref)
    xn = _rms_rows(x, g_ref[...], D_MODEL).astype(BF16)
    acc = x
    for c in range(D_FF // FF_CHUNK):
        cs = slice(c * FF_CHUNK, (c + 1) * FF_CHUNK)
        g = _dot(xn, wg_ref[:, cs])
        u = _dot(xn, wu_ref[:, cs])
        act = (g * jax.nn.sigmoid(g) * u).astype(BF16)
        acc = acc + _dot(act, wd_ref[cs, :])
    out_ref[...] = acc


def _outproj_ffn(h, o, sg, wo, ffn_norm, wg, wu, wd):
    nt = h.shape[0]
    tm = TOKEN_TILE
    row = pl.BlockSpec((tm, D_MODEL), lambda t: (t, 0))
    res = lambda a: _resident(a.shape)
    gated = sg is not None
    acts = [h, o] + ([sg] if gated else [])
    weights = [wo, ffn_norm, wg, wu, wd]
    return pl.pallas_call(
        functools.partial(_ffn_kernel, gated=gated),
        grid=(nt // tm,),
        in_specs=[row] * len(acts) + [res(w) for w in weights],
        out_specs=row,
        out_shape=jax.ShapeDtypeStruct((nt, D_MODEL), F32),
        compiler_params=_cparams(("arbitrary",)),
        name="outproj_ffn",
    )(*acts, *weights)


def _router_kernel(*refs, gated):
    if gated:
        h_ref, o_ref, sg_ref, wo_ref, g_ref, wrh_ref, wrl_ref, h_out, xn_out, comb_out = refs
    else:
        h_ref, o_ref, wo_ref, g_ref, wrh_ref, wrl_ref, h_out, xn_out, comb_out = refs
        sg_ref = None
    x = _mixer_residual(h_ref, o_ref, sg_ref, wo_ref)
    h_out[...] = x
    xn = _rms_rows(x, g_ref[...], D_MODEL)
    xh = xn.astype(BF16)
    xl = (xn - xh.astype(F32)).astype(BF16)
    xn_out[...] = xn
    logits = _dot(xh, wrh_ref[...]) + _dot(xl, wrh_ref[...]) + _dot(xh, wrl_ref[...])
    lane = _lane_iota(logits.shape)
    lane_f = lane.astype(F32)
    logits = jnp.where(lane < N_EXPERTS, logits, NEG_BIG)
    t1 = jnp.max(logits, axis=-1, keepdims=True)
    i1 = jnp.min(jnp.where(logits == t1, lane_f, float(LANES)), axis=-1, keepdims=True)
    rest = jnp.where(lane_f == i1, NEG_BIG, logits)
    t2 = jnp.max(rest, axis=-1, keepdims=True)
    i2 = jnp.min(jnp.where(rest == t2, lane_f, float(LANES)), axis=-1, keepdims=True)
    e = jnp.exp(t2 - t1)
    w1 = 1.0 / (1.0 + e)
    w2 = e / (1.0 + e)
    comb_out[...] = jnp.where(lane == 0, i1, jnp.where(lane == 1, i2, jnp.where(lane == 2, w1,
                    jnp.where(lane == 3, w2, 0.0))))


def _outproj_router(h, o, sg, wo, ffn_norm, wr_hi, wr_lo):
    nt = h.shape[0]
    tm = TOKEN_TILE
    row = lambda w: pl.BlockSpec((tm, w), lambda t: (t, 0))
    res = lambda a: _resident(a.shape)
    gated = sg is not None
    acts = [h, o] + ([sg] if gated else [])
    weights = [wo, ffn_norm, wr_hi, wr_lo]
    return pl.pallas_call(
        functools.partial(_router_kernel, gated=gated),
        grid=(nt // tm,),
        in_specs=[row(D_MODEL)] * len(acts) + [res(w) for w in weights],
        out_specs=[row(D_MODEL), row(D_MODEL), row(LANES)],
        out_shape=[jax.ShapeDtypeStruct((nt, D_MODEL), F32), jax.ShapeDtypeStruct((nt, D_MODEL), F32),
                   jax.ShapeDtypeStruct((nt, LANES), F32)],
        compiler_params=_cparams(("arbitrary",)),
        name="outproj_router",
    )(*acts, *weights)


def _row_copy(src_hbm, src_row, dst, dst_row, sem):
    return pltpu.make_async_copy(src_hbm.at[pl.ds(src_row, 1)], dst.at[pl.ds(dst_row, 1)], sem)


def _drain_rows(src_hbm, dst, sem, n):
    def wait(r, c):
        _row_copy(src_hbm, 0, dst, 0, sem).wait()
        return c
    lax.fori_loop(0, n, wait, 0, unroll=8)


def _dispatch_kernel(dest_ref, zstart_ref, xn_ref, xs_hbm, zbuf, sem, *, rows):
    t = pl.program_id(0)

    @pl.when(t == 0)
    def _():
        zbuf[...] = jnp.zeros_like(zbuf)
        for e in range(2 * N_EXPERTS):
            @pl.when(zstart_ref[e] >= 0)
            def _():
                start = pl.multiple_of(zstart_ref[e], GROUP_TILE)
                cp = pltpu.make_async_copy(zbuf, xs_hbm.at[pl.ds(start, GROUP_TILE)], sem)
                cp.start()
                cp.wait()

    base = t * rows

    def issue(r, c):
        tok = base + r
        _row_copy(xn_ref, r, xs_hbm, dest_ref[2 * tok], sem).start()
        _row_copy(xn_ref, r, xs_hbm, dest_ref[2 * tok + 1], sem).start()
        return c

    lax.fori_loop(0, rows, issue, 0, unroll=8)
    _drain_rows(xn_ref, xs_hbm, sem, 2 * rows)


def _dispatch(dest, zstart, xn, n_rows):
    nt = xn.shape[0]
    rows = ROUTE_ROWS
    return pl.pallas_call(
        functools.partial(_dispatch_kernel, rows=rows),
        grid_spec=pltpu.PrefetchScalarGridSpec(
            num_scalar_prefetch=2, grid=(nt // rows,),
            in_specs=[pl.BlockSpec((rows, D_MODEL), lambda t, dest, zstart: (t, 0))],
            out_specs=pl.BlockSpec(memory_space=pl.ANY),
            scratch_shapes=[pltpu.VMEM((GROUP_TILE, D_MODEL), F32), pltpu.SemaphoreType.DMA(())]),
        out_shape=jax.ShapeDtypeStruct((n_rows, D_MODEL), F32),
        compiler_params=_cparams(("arbitrary",)),
        name="moe_dispatch",
    )(dest, zstart, xn)


def _grouped_ffn_kernel(te_ref, ntiles_ref, xs_ref, wg_ref, wu_ref, wd_ref, ys_ref):
    used = pl.program_id(0) < ntiles_ref[0]

    @pl.when(jnp.logical_not(used))
    def _():
        ys_ref[...] = jnp.zeros_like(ys_ref)

    @pl.when(used)
    def _():
        x = xs_ref[...].astype(BF16)
        acc = None
        for c in range(D_FF // FF_CHUNK):
            cs = slice(c * FF_CHUNK, (c + 1) * FF_CHUNK)
            g = _dot(x, wg_ref[:, cs])
            u = _dot(x, wu_ref[:, cs])
            act = (g * jax.nn.sigmoid(g) * u).astype(BF16)
            part = _dot(act, wd_ref[cs, :])
            acc = part if acc is None else acc + part
        ys_ref[...] = acc


def _grouped_ffn(tile_expert, n_tiles, xs, wg, wu, wd):
    n_rows = xs.shape[0]
    tg = GROUP_TILE
    rows = pl.BlockSpec((tg, D_MODEL), lambda i, te, n: (i, 0))
    expert = lambda a: pl.BlockSpec((None,) + a.shape[1:], lambda i, te, n: (te[i], 0, 0),
                                    pipeline_mode=pl.Buffered(1))
    return pl.pallas_call(
        _grouped_ffn_kernel,
        grid_spec=pltpu.PrefetchScalarGridSpec(
            num_scalar_prefetch=2, grid=(n_rows // tg,),
            in_specs=[rows, expert(wg), expert(wu), expert(wd)],
            out_specs=rows),
        out_shape=jax.ShapeDtypeStruct((n_rows, D_MODEL), F32),
        compiler_params=_cparams(("arbitrary",)),
        name="moe_grouped_ffn",
    )(tile_expert, n_tiles, xs, wg, wu, wd)


def _combine_kernel(dest_ref, h_ref, route_ref, ys_hbm, out_ref, buf, sem, *, rows):
    base = pl.program_id(0) * rows

    def issue(r, c):
        tok = base + r
        _row_copy(ys_hbm, dest_ref[2 * tok], buf.at[0], r, sem).start()
        _row_copy(ys_hbm, dest_ref[2 * tok + 1], buf.at[1], r, sem).start()
        return c

    lax.fori_loop(0, rows, issue, 0, unroll=8)
    _drain_rows(ys_hbm, buf.at[0], sem, 2 * rows)
    route = route_ref[...]
    lane = _lane_iota(route.shape)
    w1 = jnp.sum(jnp.where(lane == 2, route, 0.0), axis=-1, keepdims=True)
    w2 = jnp.sum(jnp.where(lane == 3, route, 0.0), axis=-1, keepdims=True)
    out_ref[...] = h_ref[...] + w1 * buf[0] + w2 * buf[1]


def _combine(dest, h, route, ys, n_out):
    nt = n_out
    rows = ROUTE_ROWS
    row = lambda w: pl.BlockSpec((rows, w), lambda t, d: (t, 0))
    return pl.pallas_call(
        functools.partial(_combine_kernel, rows=rows),
        grid_spec=pltpu.PrefetchScalarGridSpec(
            num_scalar_prefetch=1, grid=(nt // rows,),
            in_specs=[row(D_MODEL), row(LANES), pl.BlockSpec(memory_space=pl.ANY)],
            out_specs=row(D_MODEL),
            scratch_shapes=[pltpu.VMEM((2, rows, D_MODEL), F32), pltpu.SemaphoreType.DMA(())]),
        out_shape=jax.ShapeDtypeStruct((nt, D_MODEL), F32),
        compiler_params=_cparams(("arbitrary",)),
        name="moe_combine",
    )(dest, h, route, ys)


def _route_plan(route):
    tg = GROUP_TILE
    n_slots = 2 * route.shape[0]
    max_tiles = n_slots // tg + N_EXPERTS
    es = route[:, :2].astype(jnp.int32).reshape(n_slots)
    onehot = (es[:, None] == jnp.arange(N_EXPERTS, dtype=jnp.int32)[None, :]).astype(jnp.int32)
    csum = jnp.cumsum(onehot, axis=0)
    counts = csum[-1]
    padded = (counts + tg - 1) // tg * tg
    gend = jnp.cumsum(padded)
    dest = jnp.sum(onehot * (csum - 1 + (gend - padded)[None, :]), axis=1)
    n_tiles = gend[-1] // tg
    tile_row = jnp.minimum(jnp.arange(max_tiles, dtype=jnp.int32), n_tiles - 1) * tg
    tile_expert = jnp.sum((tile_row[:, None] >= gend[None, :]).astype(jnp.int32), axis=1)
    tail = (n_tiles + jnp.arange(N_EXPERTS, dtype=jnp.int32)) * tg
    zstart = jnp.concatenate([jnp.where(padded > 0, gend - tg, -1), jnp.where(tail < max_tiles * tg, tail, -1)])
    return dest, zstart, tile_expert, n_tiles.reshape(1), max_tiles * tg


def _pad_heads(w, width):
    k = w.shape[0]
    w = w.reshape(k, N_HEADS, width)
    return jnp.pad(w, ((0, 0), (0, 0), (0, LANES - width))).reshape(k, N_HEADS * LANES)


def _row128(v, offset=0):
    return jnp.pad(v.astype(F32), (offset, LANES - offset - v.shape[0])).reshape(1, LANES)


def _fox_params(mix_norm, w_in, b_f, q_norm, k_norm, w_out):
    d = D_MODEL
    w_all = jnp.concatenate([
        _pad_heads(w_in[:, 0:d], HEAD_DIM), _pad_heads(w_in[:, d:2 * d], HEAD_DIM),
        _pad_heads(w_in[:, 2 * d:3 * d], HEAD_DIM), w_in[:, 3 * d:4 * d],
        jnp.pad(w_in[:, 4 * d:], ((0, 0), (0, LANES - N_HEADS)))], axis=1).astype(BF16)
    scale = HEAD_DIM ** -0.5 * LOG2E
    return (mix_norm.reshape(1, d), w_all, _row128(b_f), _row128(q_norm * scale), _row128(k_norm),
            w_out.astype(BF16))


def _mla_params(mix_norm, w_in, q_a_norm, w_q_up, kv_a_norm, w_kv_up, q_norm, k_nope_norm, k_rope_norm, w_out):
    d = D_MODEL
    lat = MLA_Q_RANK + MLA_KV_RANK
    w_in_p = jnp.concatenate([w_in[:, :lat], jnp.zeros((d, MLA_NOPE), F32), w_in[:, lat:],
                              jnp.zeros((d, LANES - MLA_NOPE - MLA_ROPE), F32)], axis=1).astype(BF16)
    scale = (MLA_NOPE + MLA_ROPE) ** -0.5 * LOG2E
    return (mix_norm.reshape(1, d), w_in_p, q_a_norm.reshape(1, -1), kv_a_norm.reshape(1, -1),
            _pad_heads(w_q_up, MLA_NOPE + MLA_ROPE).astype(BF16), w_kv_up.astype(BF16),
            _row128(q_norm * scale), _row128(k_nope_norm), _row128(k_rope_norm, MLA_NOPE),
            w_out.astype(BF16))


def _router_split(w_router):
    w = jnp.pad(w_router, ((0, 0), (0, LANES - N_EXPERTS)))
    hi = w.astype(BF16)
    return hi, (w - hi.astype(F32)).astype(BF16)


def kernel(x, positions, meta_tokens, l0_mix_norm, l0_fox_w_in, l0_fox_b_f, l0_fox_q_norm, l0_fox_k_norm, l0_fox_w_out, l0_ffn_norm, l0_ffn_w_in, l0_ffn_w_out, l1_mix_norm, l1_mla_w_in, l1_mla_q_a_norm, l1_mla_w_q_up, l1_mla_kv_a_norm, l1_mla_w_kv_up, l1_mla_q_norm, l1_mla_k_nope_norm, l1_mla_k_rope_norm, l1_mla_w_out, l1_ffn_norm, l1_moe_w_router, l1_moe_w_in, l1_moe_w_out, l2_mix_norm, l2_fox_w_in, l2_fox_b_f, l2_fox_q_norm, l2_fox_k_norm, l2_fox_w_out, l2_ffn_norm, l2_ffn_w_in, l2_ffn_w_out, l3_mix_norm, l3_mla_w_in, l3_mla_q_a_norm, l3_mla_w_q_up, l3_mla_kv_a_norm, l3_mla_w_kv_up, l3_mla_q_norm, l3_mla_k_nope_norm, l3_mla_k_rope_norm, l3_mla_w_out, l3_ffn_norm, l3_moe_w_router, l3_moe_w_in, l3_moe_w_out):
    batch, seq, d = x.shape
    assert d == D_MODEL and seq % Q_TILE == 0 and seq % TOKEN_TILE == 0 and META_PAD % TOKEN_TILE == 0
    n_real = batch * seq
    pad_rows = META_PAD - N_META

    h = jnp.concatenate([x.reshape(n_real, d), meta_tokens.astype(x.dtype), jnp.zeros((pad_rows, d), x.dtype)], axis=0)
    pos = jnp.concatenate([positions.reshape(n_real), jnp.arange(N_META, dtype=positions.dtype),
                           jnp.zeros((pad_rows,), positions.dtype)]).astype(F32)
    pos_b = jnp.broadcast_to(pos[:, None], (n_real + META_PAD, LANES))
    inv_freq = ROPE_THETA ** (-jnp.arange(0, MLA_ROPE, 2, dtype=F32) / MLA_ROPE)
    half = MLA_ROPE // 2
    freq_p = _row128(jnp.concatenate([inv_freq, inv_freq]), MLA_NOPE)
    sign_p = _row128(jnp.concatenate([-jnp.ones((half,), F32), jnp.ones((half,), F32)]), MLA_NOPE)
    cos_p, sin_p = _rope_tables(pos_b, freq_p, sign_p)

    def fox(h, params):
        mix_norm, w_all, b_f, qg, kg, w_out = _fox_params(*params)
        q2, k2, v2, sg = _fox_pre(h, mix_norm, w_all, b_f, qg, kg, seq=seq)
        return _attention(q2, k2, v2, batch=batch, seq=seq), sg, w_out

    def mla(h, params):
        *pre, w_out = _mla_params(*params)
        q2, k2, v2 = _mla_pre(h, *pre, cos_p, sin_p)
        return _attention(q2, k2, v2, batch=batch, seq=seq), None, w_out

    def dense(h, o, sg, wo, ffn_norm, w_in, w_out):
        return _outproj_ffn(h, o, sg, wo, ffn_norm.reshape(1, d), w_in[:, :D_FF].astype(BF16),
                            w_in[:, D_FF:].astype(BF16), w_out.astype(BF16))

    def moe(h, o, sg, wo, ffn_norm, w_router, w_in, w_out, n_out):
        wr_hi, wr_lo = _router_split(w_router)
        h, xn, route = _outproj_router(h, o, sg, wo, ffn_norm.reshape(1, d), wr_hi, wr_lo)
        dest, zstart, tile_expert, n_tiles, n_rows = _route_plan(route)
        xs = _dispatch(dest, zstart, xn, n_rows)
        ys = _grouped_ffn(tile_expert, n_tiles, xs, w_in[:, :, :D_FF].astype(BF16), w_in[:, :, D_FF:].astype(BF16),
                          w_out.astype(BF16))
        return _combine(dest, h, route, ys, n_out)

    o, sg, wo = fox(h, (l0_mix_norm, l0_fox_w_in, l0_fox_b_f, l0_fox_q_norm, l0_fox_k_norm, l0_fox_w_out))
    h = dense(h, o, sg, wo, l0_ffn_norm, l0_ffn_w_in, l0_ffn_w_out)
    o, sg, wo = mla(h, (l1_mix_norm, l1_mla_w_in, l1_mla_q_a_norm, l1_mla_w_q_up, l1_mla_kv_a_norm, l1_mla_w_kv_up,
                        l1_mla_q_norm, l1_mla_k_nope_norm, l1_mla_k_rope_norm, l1_mla_w_out))
    h = moe(h, o, sg, wo, l1_ffn_norm, l1_moe_w_router, l1_moe_w_in, l1_moe_w_out, n_real + META_PAD)
    o, sg, wo = fox(h, (l2_mix_norm, l2_fox_w_in, l2_fox_b_f, l2_fox_q_norm, l2_fox_k_norm, l2_fox_w_out))
    h = dense(h, o, sg, wo, l2_ffn_norm, l2_ffn_w_in, l2_ffn_w_out)
    o, sg, wo = mla(h, (l3_mix_norm, l3_mla_w_in, l3_mla_q_a_norm, l3_mla_w_q_up, l3_mla_kv_a_norm, l3_mla_w_kv_up,
                        l3_mla_q_norm, l3_mla_k_nope_norm, l3_mla_k_rope_norm, l3_mla_w_out))
    out = moe(h, o, sg, wo, l3_ffn_norm, l3_moe_w_router, l3_moe_w_in, l3_moe_w_out, n_real)
    return out.reshape(batch, seq, d)
```
